```python
import math
import jax, jax.numpy as jnp
from jax import lax
import numpy as np

D_MODEL = 4096
BATCH = 4
SEQ = 2048
DEPTH = 1
DEC_BATCH = 128
DEC_SEQ = 1
PAST_LEN = 2048
PAGE_SIZE = 128

POOL_WIDTH = D_MODEL // 2
POOL_WINDOWS = (2, 4, 8, 16)
POOL_GROUP = POOL_WIDTH // len(POOL_WINDOWS)
POOL_STATE = max(POOL_WINDOWS) - 1
ATTN_WIDTH = D_MODEL - POOL_WIDTH
DV = 128
N_HEADS = ATTN_WIDTH // DV
DQK = DV // 2
IN_WIDTH = POOL_WIDTH + 3 * ATTN_WIDTH
QBLOCK = 128
NUM_BUCKETS = 32
MAX_DISTANCE = 128
PEER_HEADS = 8
N_KEYS = 128
N_EXPERTS = N_KEYS * N_KEYS
PEER_TOPK = 16
D_KEY = 256
D_HALF = D_KEY // 2
PEER_BLOCK = 128
EPS = 1e-6

kernel_name = 'hymba_pool_diffattn_peer_step'


def rmsnorm(x, g):
    xf = x.astype(jnp.float32)
    y = xf * lax.rsqrt(jnp.mean(xf * xf, axis=-1, keepdims=True) + EPS)
    return (y * g.astype(jnp.float32)).astype(x.dtype)


def rel_bucket(rel):
    n = jnp.maximum(-rel, 0)
    max_exact = NUM_BUCKETS // 2
    nf = jnp.maximum(n, 1).astype(jnp.float32)
    large = max_exact + (jnp.log(nf / max_exact) / math.log(MAX_DISTANCE / max_exact)
                         * (NUM_BUCKETS - max_exact)).astype(jnp.int32)
    large = jnp.minimum(large, NUM_BUCKETS - 1)
    return jnp.where(n < max_exact, n, large)


def rel_bias_bhqk(rel, rel_bias):
    b = rel_bias[rel_bucket(rel)].astype(jnp.float32)
    return jnp.transpose(b, (2, 0, 1))[None, :, None]


def project(x, ln1_g, w_in, qn_g, kn_g):
    B, T, _ = x.shape
    h = rmsnorm(x, ln1_g) @ w_in
    u = h[..., :POOL_WIDTH]
    q = h[..., POOL_WIDTH:POOL_WIDTH + ATTN_WIDTH].reshape(B, T, N_HEADS, 2, DQK)
    k = h[..., POOL_WIDTH + ATTN_WIDTH:POOL_WIDTH + 2 * ATTN_WIDTH].reshape(B, T, N_HEADS, 2, DQK)
    v = h[..., POOL_WIDTH + 2 * ATTN_WIDTH:].reshape(B, T, N_HEADS, DV)
    return u, rmsnorm(q, qn_g), rmsnorm(k, kn_g), v


def pool_mix(u, prefix, start_pos, w_pool, pool_scale):
    B, T, _ = u.shape
    L = prefix.shape[1]
    full = jnp.concatenate([prefix, u], axis=1).astype(jnp.float32)
    csum = jnp.concatenate([jnp.zeros((B, 1, POOL_WIDTH), jnp.float32),
                            jnp.cumsum(full, axis=1)], axis=1)
    pos = start_pos + jnp.arange(T)
    uf = u.astype(jnp.float32)
    outs = []
    for gi, w in enumerate(POOL_WINDOWS):
        c = slice(gi * POOL_GROUP, (gi + 1) * POOL_GROUP)
        win = csum[:, L + 1:L + T + 1, c] - csum[:, L + 1 - w:L + T + 1 - w, c]
        cnt = jnp.minimum(w, pos + 1).astype(jnp.float32)[None, :, None]
        d = win / cnt - uf[..., c]
        outs.append(jnp.einsum('btc,cd->btd', d, w_pool[gi].astype(jnp.float32)))
    y = jnp.concatenate(outs, axis=-1) * pool_scale.astype(jnp.float32)
    return y.astype(u.dtype)


def diff_weights(s, rel, lam, rel_bias):
    s = s.astype(jnp.float32) * (DQK ** -0.5) + rel_bias_bhqk(rel, rel_bias)
    s = jnp.where(rel <= 0, s, -jnp.inf)
    p = jax.nn.softmax(s, axis=-1)
    return p[:, :, 0] - lam * p[:, :, 1]


def diff_attn_prompt(q, k, v, lam, rel_bias):
    B, S = q.shape[:2]
    nb = S // QBLOCK
    qb = jnp.moveaxis(q.reshape(B, nb, QBLOCK, N_HEADS, 2, DQK), 1, 0)
    kpos = jnp.arange(S)
    vf = v.astype(jnp.float32)

    def blk(args):
        i, qi = args
        qpos = i * QBLOCK + jnp.arange(QBLOCK)
        s = jnp.einsum('bqhmd,bkhmd->bhmqk', qi, k)
        w = diff_weights(s, kpos[None, :] - qpos[:, None], lam, rel_bias)
        return jnp.einsum('bhqk,bkhd->bqhd', w, vf)

    o = lax.map(blk, (jnp.arange(nb), qb))
    return jnp.moveaxis(o, 0, 1).reshape(B, S, N_HEADS, DV)


def diff_attn_sample(l, q, k_new, v_new, cache_k, cache_v, page_table, lam, rel_bias):
    Bd, Tq = q.shape[:2]
    n_pages = page_table.shape[1]
    past = n_pages * PAGE_SIZE

    def page_scores(j):
        kp = cache_k[l, page_table[:, j]].reshape(Bd, PAGE_SIZE, N_HEADS, 2, DQK)
        return jnp.einsum('bqhmd,bkhmd->bhmqk', q, kp.astype(q.dtype))

    s_past = lax.map(page_scores, jnp.arange(n_pages))
    s_past = jnp.moveaxis(s_past, 0, 4).reshape(Bd, N_HEADS, 2, Tq, past)
    s_new = jnp.einsum('bqhmd,bkhmd->bhmqk', q, k_new)
    s = jnp.concatenate([s_past.astype(jnp.float32), s_new.astype(jnp.float32)], axis=-1)
    qpos = past + jnp.arange(Tq)
    kpos = jnp.arange(past + Tq)
    w = diff_weights(s, kpos[None, :] - qpos[:, None], lam, rel_bias)
    w_past = jnp.moveaxis(w[..., :past].reshape(Bd, N_HEADS, Tq, n_pages, PAGE_SIZE), 3, 0)
    acc0 = jnp.einsum('bhqk,bkhd->bqhd', w[..., past:], v_new.astype(jnp.float32))

    def body(acc, xs):
        j, wj = xs
        vp = cache_v[l, page_table[:, j]].astype(jnp.float32)
        return acc + jnp.einsum('bhqk,bkhd->bqhd', wj, vp), None

    o, _ = lax.scan(body, acc0, (jnp.arange(n_pages), w_past))
    return o


def mix_out(x, pool_y, attn_o, lam_init, subln_g, w_o):
    B, T, _ = x.shape
    a = (rmsnorm(attn_o, subln_g) * (1.0 - lam_init)).reshape(B, T, ATTN_WIDTH).astype(x.dtype)
    m = jnp.concatenate([pool_y.astype(x.dtype), a], axis=-1)
    return x + m @ w_o


def peer_experts(h, idx, g, peer_u, peer_v):
    N, D = h.shape
    nb = -(-N // PEER_BLOCK)
    pad = nb * PEER_BLOCK - N
    h = jnp.pad(h, ((0, pad), (0, 0))).reshape(nb, PEER_BLOCK, D)
    idx = jnp.pad(idx, ((0, pad), (0, 0))).reshape(nb, PEER_BLOCK, -1)
    g = jnp.pad(g, ((0, pad), (0, 0))).reshape(nb, PEER_BLOCK, -1)

    def blk(args):
        hb, ib, gb = args
        ue = peer_u[ib]
        a = jax.nn.gelu(jnp.einsum('td,ted->te', hb, ue).astype(jnp.float32), approximate=False) * gb
        ve = peer_v[ib]
        return jnp.einsum('te,ted->td', a.astype(ve.dtype), ve)

    out = lax.map(blk, (h, idx, g)).reshape(nb * PEER_BLOCK, D)
    return out[:N]


def peer_ffn(x, ln2_g, w_query, sub_keys, peer_u, peer_v):
    B, T, D = x.shape
    h = rmsnorm(x, ln2_g).reshape(B * T, D)
    qr = (h @ w_query).reshape(B * T, PEER_HEADS, 2, D_HALF)
    s = jnp.einsum('thcd,hcnd->thcn', qr, sub_keys).astype(jnp.float32)
    s1, i1 = lax.top_k(s[:, :, 0], PEER_TOPK)
    s2, i2 = lax.top_k(s[:, :, 1], PEER_TOPK)
    cand = (s1[..., :, None] + s2[..., None, :]).reshape(B * T, PEER_HEADS, PEER_TOPK * PEER_TOPK)
    top_s, top_c = lax.top_k(cand, PEER_TOPK)
    e1 = jnp.take_along_axis(i1, top_c // PEER_TOPK, axis=-1)
    e2 = jnp.take_along_axis(i2, top_c % PEER_TOPK, axis=-1)
    idx = (e1 * N_KEYS + e2).reshape(B * T, PEER_HEADS * PEER_TOPK)
    g = jax.nn.softmax(top_s, axis=-1).reshape(B * T, PEER_HEADS * PEER_TOPK)
    out = peer_experts(h, idx, g, peer_u, peer_v)
    return x + out.reshape(B, T, D).astype(x.dtype)


def setup_inputs(seed: int = 0) -> dict:
    key = jax.random.key(seed)
    ks = jax.random.split(key, 24)
    f32 = jnp.float32
    n_pages = PAST_LEN // PAGE_SIZE
    n_used = DEC_BATCH * n_pages
    n_pool = n_used + max(n_used // 4, 1)

    def nrm(k, shape, s):
        return jax.random.normal(k, shape, f32) * s

    def gain(k, shape):
        return 1.0 + 0.02 * jax.random.normal(k, shape, f32)

    page_table = jax.random.permutation(ks[5], n_pool)[:n_used].reshape(DEC_BATCH, n_pages).astype(jnp.int32)
    return {
        'x_prompt': nrm(ks[0], (BATCH, SEQ, D_MODEL), 1.0),
        'x_sample': nrm(ks[1], (DEC_BATCH, DEC_SEQ, D_MODEL), 1.0),
        'cache_k': nrm(ks[2], (DEPTH, n_pool, PAGE_SIZE, N_HEADS, 2 * DQK), 1.0),
        'cache_v': nrm(ks[3], (DEPTH, n_pool, PAGE_SIZE, N_HEADS, DV), 1.0),
        'state_pool': nrm(ks[4], (DEPTH, DEC_BATCH, POOL_STATE, POOL_WIDTH), 1.0),
        'page_table': page_table,
        'rel_bias': nrm(ks[6], (NUM_BUCKETS, N_HEADS), 0.5),
        'ln1_g': gain(ks[7], (DEPTH, D_MODEL)),
        'w_in': nrm(ks[8], (DEPTH, D_MODEL, IN_WIDTH), D_MODEL ** -0.5),
        'q_norm_g': gain(ks[9], (DEPTH, DQK)),
        'k_norm_g': gain(ks[10], (DEPTH, DQK)),
        'lambda_q1': nrm(ks[11], (DEPTH, DQK), 0.1),
        'lambda_k1': nrm(ks[12], (DEPTH, DQK), 0.1),
        'lambda_q2': nrm(ks[13], (DEPTH, DQK), 0.1),
        'lambda_k2': nrm(ks[14], (DEPTH, DQK), 0.1),
        'subln_g': gain(ks[15], (DEPTH, DV)),
        'w_pool': nrm(ks[16], (DEPTH, len(POOL_WINDOWS), POOL_GROUP, POOL_GROUP), POOL_GROUP ** -0.5),
        'pool_scale': gain(ks[17], (DEPTH, POOL_WIDTH)),
        'w_o': nrm(ks[18], (DEPTH, D_MODEL, D_MODEL), D_MODEL ** -0.5),
        'ln2_g': gain(ks[19], (DEPTH, D_MODEL)),
        'w_query': nrm(ks[20], (DEPTH, D_MODEL, PEER_HEADS * D_KEY), D_MODEL ** -0.5),
        'sub_keys': nrm(ks[21], (DEPTH, PEER_HEADS, 2, N_KEYS, D_HALF), D_HALF ** -0.5),
        'peer_u': nrm(ks[22], (DEPTH, N_EXPERTS, D_MODEL), D_MODEL ** -0.5),
        'peer_v': nrm(ks[23], (DEPTH, N_EXPERTS, D_MODEL), PEER_HEADS ** -0.5),
    }


def reference(x_prompt, x_sample, cache_k, cache_v, state_pool, page_table, rel_bias,
              ln1_g, w_in, q_norm_g, k_norm_g, lambda_q1, lambda_k1, lambda_q2, lambda_k2,
              subln_g, w_pool, pool_scale, w_o, ln2_g, w_query, sub_keys, peer_u, peer_v):
    B, S, _ = x_prompt.shape
    Bd, Tq, _ = x_sample.shape
    past = page_table.shape[1] * PAGE_SIZE
    xp, xs = x_prompt, x_sample
    kp_l, vp_l, pp_l, ks_l, vs_l, ps_l = [], [], [], [], [], []
    for l in range(DEPTH):
        lam_init = 0.8 - 0.6 * math.exp(-0.3 * l)
        lam = (jnp.exp(jnp.sum(lambda_q1[l].astype(jnp.float32) * lambda_k1[l].astype(jnp.float32)))
               - jnp.exp(jnp.sum(lambda_q2[l].astype(jnp.float32) * lambda_k2[l].astype(jnp.float32)))
               + lam_init)
        u, q, k, v = project(xp, ln1_g[l], w_in[l], q_norm_g[l], k_norm_g[l])
        prefix = jnp.zeros((B, POOL_STATE, POOL_WIDTH), u.dtype)
        pool_y = pool_mix(u, prefix, 0, w_pool[l], pool_scale[l])
        att = diff_attn_prompt(q, k, v, lam, rel_bias)
        xp = mix_out(xp, pool_y, att, lam_init, subln_g[l], w_o[l])
        xp = peer_ffn(xp, ln2_g[l], w_query[l], sub_keys[l], peer_u[l], peer_v[l])
        kp_l.append(k.reshape(B, S, N_HEADS, 2 * DQK))
        vp_l.append(v)
        pp_l.append(jnp.concatenate([prefix, u], axis=1)[:, -POOL_STATE:])
        u, q, k, v = project(xs, ln1_g[l], w_in[l], q_norm_g[l], k_norm_g[l])
        prefix = state_pool[l].astype(u.dtype)
        pool_y = pool_mix(u, prefix, past, w_pool[l], pool_scale[l])
        att = diff_attn_sample(l, q, k, v, cache_k, cache_v, page_table, lam, rel_bias)
        xs = mix_out(xs, pool_y, att, lam_init, subln_g[l], w_o[l])
        xs = peer_ffn(xs, ln2_g[l], w_query[l], sub_keys[l], peer_u[l], peer_v[l])
        ks_l.append(k.reshape(Bd, Tq, N_HEADS, 2 * DQK))
        vs_l.append(v)
        ps_l.append(jnp.concatenate([prefix, u], axis=1)[:, -POOL_STATE:])
    k_prompt = jnp.stack(kp_l)
    v_prompt = jnp.stack(vp_l)
    pool_prompt = jnp.stack(pp_l)
    k_sample = jnp.stack(ks_l)
    v_sample = jnp.stack(vs_l)
    pool_sample = jnp.stack(ps_l)
    return (xp, xs, k_prompt, v_prompt, pool_prompt, k_sample, v_sample, pool_sample)
```

```python
import functools
import math

import jax
import jax.numpy as jnp
from jax import lax
from jax.experimental import pallas as pl
from jax.experimental.pallas import tpu as pltpu

F32 = jnp.float32
BF16 = jnp.bfloat16
EPS = 1e-6

LANES = 128
VMEM_LIMIT_BYTES = 56 * 1024 * 1024

POOL_WINDOWS = (2, 4, 8, 16)
DQK = 64
DV = 128
NUM_BUCKETS = 32
MAX_DISTANCE = 128
PEER_TOPK = 16
PAGE_SIZE = 128


def _params(*sem):
    return pltpu.CompilerParams(dimension_semantics=sem, vmem_limit_bytes=VMEM_LIMIT_BYTES)


def _split_bf16(x):
    hi = x.astype(BF16)
    lo = (x - hi.astype(F32)).astype(BF16)
    return hi, lo


def _dot(a, b):
    return jnp.dot(a, b, preferred_element_type=F32)


def _dot2(x, w):
    hi, lo = _split_bf16(x)
    return _dot(hi, w) + _dot(lo, w)


def _rmsnorm_body(x_ref, g_ref, o_ref):
    x = x_ref[...]
    ms = jnp.mean(x * x, axis=-1, keepdims=True)
    o_ref[...] = (x * lax.rsqrt(ms + EPS) * g_ref[...]).astype(o_ref.dtype)


def _rmsnorm(x, g, tm):
    m, d = x.shape
    return pl.pallas_call(
        _rmsnorm_body,
        grid=(m // tm,),
        in_specs=[pl.BlockSpec((tm, d), lambda i: (i, 0)),
                  pl.BlockSpec((1, d), lambda i: (0, 0))],
        out_specs=pl.BlockSpec((tm, d), lambda i: (i, 0)),
        out_shape=jax.ShapeDtypeStruct((m, d), BF16),
        compiler_params=_params("parallel"),
        name="rmsnorm",
    )(x, g.reshape(1, d))


def _mm_body(*refs, n_pairs, qk_group, has_res):
    a_refs = refs[:n_pairs]
    w_refs = refs[n_pairs:2 * n_pairs]
    pos = 2 * n_pairs
    if qk_group:
        g_ref, ones_ref = refs[pos], refs[pos + 1]
        pos += 2
    if has_res:
        r_ref = refs[pos]
        pos += 1
    o_ref = refs[pos]

    acc = _dot(a_refs[0][...], w_refs[0][...])
    for p in range(1, n_pairs):
        acc = acc + _dot(a_refs[p][...], w_refs[p][...])
    if qk_group:
        sq = acc * acc
        hi, lo = _split_bf16(sq)
        ones = ones_ref[...]
        parts = []
        for s in range(acc.shape[1] // LANES):
            sl = slice(s * LANES, (s + 1) * LANES)
            parts.append(_dot(hi[:, sl], ones) + _dot(lo[:, sl], ones))
        gs = jnp.concatenate(parts, axis=1)
        acc = acc * lax.rsqrt(gs * (1.0 / qk_group) + EPS) * g_ref[...]
    if has_res:
        acc = acc + r_ref[...]
    o_ref[...] = acc.astype(o_ref.dtype)


def _matmul(a_list, w_list, *, n_out, tm, tn, out_dtype, w_col_off=0,
            gain=None, qk_group=0, res=None, name="matmul"):
    m = a_list[0].shape[0]
    tn = _tile(n_out, tn)
    off = w_col_off // tn
    in_specs, args = [], []
    for a in a_list:
        in_specs.append(pl.BlockSpec((tm, a.shape[1]), lambda i, j: (i, 0)))
        args.append(a)
    for w, rb, kr in w_list:
        in_specs.append(pl.BlockSpec((kr, tn), lambda i, j, rb=rb: (rb, j + off)))
        args.append(w)
    if qk_group:
        lane = jnp.arange(LANES)
        ones_bd = (lane[:, None] // qk_group == lane[None, :] // qk_group).astype(BF16)
        in_specs += [pl.BlockSpec((1, tn), lambda i, j: (0, 0)),
                     pl.BlockSpec((LANES, LANES), lambda i, j: (0, 0))]
        args += [jnp.tile(gain.astype(F32), tn // gain.shape[0]).reshape(1, tn), ones_bd]
    if res is not None:
        in_specs.append(pl.BlockSpec((tm, tn), lambda i, j: (i, j)))
        args.append(res)
    body = functools.partial(_mm_body, n_pairs=len(a_list), qk_group=qk_group,
                             has_res=res is not None)
    return pl.pallas_call(
        body,
        grid=(m // tm, n_out // tn),
        in_specs=in_specs,
        out_specs=pl.BlockSpec((tm, tn), lambda i, j: (i, j)),
        out_shape=jax.ShapeDtypeStruct((m, n_out), out_dtype),
        compiler_params=_params("parallel", "arbitrary"),
        name=name,
    )(*args)


_POOL_PAD = 16
_POOL_ROWS = 256


def _pool_prompt_body(u_ref, w_ref, sc_ref, o_ref, pad_ref, *, seq):
    g = pl.program_id(1)
    pad_ref[0:_POOL_PAD, :] = jnp.zeros((_POOL_PAD, pad_ref.shape[1]), F32)
    pad_ref[_POOL_PAD:, :] = u_ref[...]
    wmat = w_ref[0].astype(BF16)
    scale = sc_ref[...]
    for gi, win in enumerate(POOL_WINDOWS):
        @pl.when(g == gi)
        def _(win=win):
            for r0 in range(0, seq, _POOL_ROWS):
                cur = pad_ref[r0 + _POOL_PAD:r0 + _POOL_PAD + _POOL_ROWS, :]
                tot = cur
                for i in range(1, win):
                    tot = tot + pad_ref[r0 + _POOL_PAD - i:r0 + _POOL_PAD - i + _POOL_ROWS, :]
                pos = r0 + lax.broadcasted_iota(jnp.int32, (_POOL_ROWS, 1), 0)
                cnt = jnp.minimum(win, pos + 1).astype(F32)
                d = tot / cnt - cur
                y = _dot(d.astype(BF16), wmat) * scale
                o_ref[r0:r0 + _POOL_ROWS, :] = y.astype(o_ref.dtype)


def _pool_prompt(u, w_pool, pool_scale, batch, seq):
    n, width = u.shape
    ng = len(POOL_WINDOWS)
    gw = width // ng
    body = functools.partial(_pool_prompt_body, seq=seq)
    return pl.pallas_call(
        body,
        grid=(batch, ng),
        in_specs=[pl.BlockSpec((seq, gw), lambda b, g: (b, g)),
                  pl.BlockSpec((1, gw, gw), lambda b, g: (g, 0, 0)),
                  pl.BlockSpec((1, gw), lambda b, g: (0, g))],
        out_specs=pl.BlockSpec((seq, gw), lambda b, g: (b, g)),
        out_shape=jax.ShapeDtypeStruct((n, width), BF16),
        scratch_shapes=[pltpu.VMEM((seq + _POOL_PAD, gw), F32)],
        compiler_params=_params("parallel", "arbitrary"),
        name="pool_prompt",
    )(u, w_pool, pool_scale.reshape(1, width))


def _pool_sample_body(st_ref, u_ref, w_ref, sc_ref, o_ref, *, n_state, width):
    ng = len(POOL_WINDOWS)
    gw = width // ng
    for gi, win in enumerate(POOL_WINDOWS):
        c0 = gi * gw
        cur = u_ref[:, c0:c0 + gw]
        tot = cur
        for i in range(1, win):
            r = n_state - i
            tot = tot + st_ref[:, r * width + c0:r * width + c0 + gw]
        d = tot / float(win) - cur
        y = _dot(d.astype(BF16), w_ref[gi].astype(BF16)) * sc_ref[:, c0:c0 + gw]
        o_ref[:, c0:c0 + gw] = y.astype(o_ref.dtype)


def _pool_sample(state, u, w_pool, pool_scale):
    bd, n_state, width = state.shape
    body = functools.partial(_pool_sample_body, n_state=n_state, width=width)
    return pl.pallas_call(
        body,
        out_shape=jax.ShapeDtypeStruct((bd, width), BF16),
        compiler_params=pltpu.CompilerParams(vmem_limit_bytes=VMEM_LIMIT_BYTES),
        name="pool_sample",
    )(state.reshape(bd, n_state * width), u, w_pool, pool_scale.reshape(1, width))


def _rel_bucket(n):
    max_exact = NUM_BUCKETS // 2
    nf = jnp.maximum(n, 1).astype(F32)
    large = max_exact + (jnp.log(nf / max_exact) / math.log(MAX_DISTANCE / max_exact)
                         * (NUM_BUCKETS - max_exact)).astype(jnp.int32)
    large = jnp.minimum(large, NUM_BUCKETS - 1)
    return jnp.where(n < max_exact, n, large)


def _bias_by_distance(rel_bias, n_dist):
    return rel_bias.astype(F32)[_rel_bucket(jnp.arange(n_dist))]


def _lambda(lv_ref, lam_init):
    lv = lv_ref[...]
    a = jnp.sum(lv[0:1] * lv[1:2], axis=-1, keepdims=True)
    b = jnp.sum(lv[2:3] * lv[3:4], axis=-1, keepdims=True)
    return jnp.exp(a) - jnp.exp(b) + lam_init


def _attn_prompt_body(lv_ref, q_ref, k_ref, v_ref, bias_ref, g_ref, o_ref,
                      m_sc, l_sc, acc_sc, *, tq, tk, lam_init):
    qi = pl.program_id(2)
    q = q_ref[...] * (DQK ** -0.5)
    lane = lax.broadcasted_iota(jnp.int32, q.shape, 1)
    q_ext = jnp.concatenate([jnp.where(lane < DQK, q, 0.0),
                             jnp.where(lane >= DQK, q, 0.0)], axis=0).astype(BF16)
    m_sc[...] = jnp.full(m_sc.shape, -jnp.inf, F32)
    l_sc[...] = jnp.zeros(l_sc.shape, F32)
    acc_sc[...] = jnp.zeros(acc_sc.shape, F32)

    def step(j, carry):
        r0 = pl.multiple_of(j * tk, tk)
        kj = k_ref[pl.ds(r0, tk), :].astype(BF16)
        vj = v_ref[pl.ds(r0, tk), :].astype(BF16)
        s = lax.dot_general(q_ext, kj, (((1,), (1,)), ((), ())),
                            preferred_element_type=F32)
        bt = bias_ref[0, jnp.minimum(qi - j, 2)]
        s = (s.reshape(2, tq, tk) + bt[None]).reshape(2 * tq, tk)
        m_old = m_sc[...]
        m_new = jnp.maximum(m_old, jnp.max(s, axis=1, keepdims=True))
        alpha = jnp.exp(m_old - m_new)
        p = jnp.exp(s - m_new)
        l_sc[...] = alpha * l_sc[...] + jnp.sum(p, axis=1, keepdims=True)
        acc_sc[...] = alpha * acc_sc[...] + _dot(p.astype(BF16), vj)
        m_sc[...] = m_new
        return carry

    lax.fori_loop(0, qi + 1, step, 0)

    lam = _lambda(lv_ref, lam_init)
    o = acc_sc[...] / l_sc[...]
    att = o[:tq] - lam * o[tq:]
    ms = jnp.mean(att * att, axis=-1, keepdims=True)
    out = att * lax.rsqrt(ms + EPS) * g_ref[...] * (1.0 - lam_init)
    o_ref[...] = out.astype(o_ref.dtype)


def _attn_prompt(q, k, v, lam_vecs, rel_bias, subln_g, batch, seq, lam_init, tq=256):
    n, width = q.shape
    heads = width // DV
    tk = tq
    nq = seq // tq
    bd = _bias_by_distance(rel_bias, 3 * tq)
    r = jnp.arange(tq)[:, None]
    c = jnp.arange(tk)[None, :]
    tiles = []
    for t in range(3):
        dist = t * tq + r - c
        bt = jnp.transpose(bd[jnp.clip(dist, 0, 3 * tq - 1)], (2, 0, 1))
        tiles.append(jnp.where(dist >= 0, bt, -jnp.inf))
    bias_tiles = jnp.stack(tiles, axis=1)
    body = functools.partial(_attn_prompt_body, tq=tq, tk=tk, lam_init=lam_init)
    return pl.pallas_call(
        body,
        grid=(batch, heads, nq),
        in_specs=[pl.BlockSpec((4, DQK), lambda b, h, i: (0, 0)),
                  pl.BlockSpec((tq, DV), lambda b, h, i: (b * nq + i, h)),
                  pl.BlockSpec((seq, DV), lambda b, h, i: (b, h)),
                  pl.BlockSpec((seq, DV), lambda b, h, i: (b, h)),
                  pl.BlockSpec((1, 3, tq, tk), lambda b, h, i: (h, 0, 0, 0)),
                  pl.BlockSpec((1, DV), lambda b, h, i: (0, 0))],
        out_specs=pl.BlockSpec((tq, DV), lambda b, h, i: (b * nq + i, h)),
        out_shape=jax.ShapeDtypeStruct((n, width), BF16),
        scratch_shapes=[pltpu.VMEM((2 * tq, 1), F32), pltpu.VMEM((2 * tq, 1), F32),
                        pltpu.VMEM((2 * tq, DV), F32)],
        compiler_params=_params("parallel", "parallel", "arbitrary"),
        name="attn_prompt",
    )(lam_vecs, q, k, v, bias_tiles, subln_g.reshape(1, DV))


_PAGES_PER_STEP = 4


def _attn_decode_body(pt_ref, lv_ref, q_ref, kn_ref, vn_ref, bias_ref, bias0_ref,
                      sel_ref, exp_ref, *rest, n_pages, lam_init):
    pp = _PAGES_PER_STEP
    k_refs = rest[:pp]
    v_refs = rest[pp:2 * pp]
    o_ref = rest[2 * pp]
    s_sc, pn_sc, acc_sc = rest[2 * pp + 1:]
    del pt_ref
    nk = n_pages // pp
    s = pl.program_id(1)
    q = q_ref[0] * (DQK ** -0.5)
    rows = pp * PAGE_SIZE

    @pl.when(s < nk)
    def _scores():
        prod = jnp.concatenate([k_refs[r][0] * q for r in range(pp)], axis=0)
        sc = _dot2(prod, sel_ref[...])
        r0 = pl.multiple_of(s * rows, rows)
        s_sc[pl.ds(r0, rows), :] = sc + bias_ref[pl.ds(r0, rows), :]

    @pl.when(s == nk - 1)
    def _softmax():
        prod_n = jnp.broadcast_to(kn_ref[0] * q, (8, q.shape[1]))
        sn = _dot2(prod_n, sel_ref[...])[0:1] + bias0_ref[...]
        sa = s_sc[...]
        m = jnp.maximum(jnp.max(sa, axis=0, keepdims=True), sn)
        p = jnp.exp(sa - m)
        pn = jnp.exp(sn - m)
        l = jnp.sum(p, axis=0, keepdims=True) + pn
        lam = _lambda(lv_ref, lam_init)
        col = lax.broadcasted_iota(jnp.int32, l.shape, 1)
        colscale = jnp.where(col % 2 == 1, lam, 1.0) / l
        s_sc[...] = p * colscale
        pn_sc[...] = jnp.broadcast_to(pn * colscale, pn_sc.shape)
        acc_sc[...] = jnp.zeros(acc_sc.shape, F32)

    @pl.when(s >= nk)
    def _values():
        r0 = pl.multiple_of((s - nk) * rows, rows)
        wexp = _dot2(s_sc[pl.ds(r0, rows), :], exp_ref[...])
        vals = jnp.concatenate([v_refs[r][0] for r in range(pp)], axis=0)
        contrib = wexp * vals
        acc_sc[...] += jnp.sum(contrib.reshape(rows // 8, 8, contrib.shape[1]), axis=0)

    @pl.when(s == 2 * nk - 1)
    def _finish():
        wn = _dot2(pn_sc[...], exp_ref[...])[0:1]
        o_ref[0] = jnp.sum(acc_sc[...], axis=0, keepdims=True) + wn * vn_ref[0]


def _attn_decode(q, k_new, v_new, cache_k_l, cache_v_l, page_table, lam_vecs, rel_bias,
                 lam_init):
    bd, width = q.shape
    heads = width // DV
    n_pages = page_table.shape[1]
    past = n_pages * PAGE_SIZE
    pp = _PAGES_PER_STEP
    nk = n_pages // pp
    n_pool = cache_k_l.shape[0]
    ck = cache_k_l.reshape(n_pool, PAGE_SIZE, width)
    cv = cache_v_l.reshape(n_pool, PAGE_SIZE, width)
    bdist = _bias_by_distance(rel_bias, past + 1)
    bias_cols = jnp.repeat(bdist, 2, axis=1)
    bias_cols = jnp.pad(bias_cols, ((0, 0), (0, LANES - 2 * heads)))
    bias_past = bias_cols[past - jnp.arange(past)]
    bias_new = bias_cols[0:1]
    lane = jnp.arange(width)
    col = jnp.arange(LANES)
    sel = (lane[:, None] // DQK == col[None, :]).astype(BF16)
    sign = jnp.where(col % 2 == 1, -1.0, 1.0)[:, None]
    expand = ((col[:, None] // 2 == lane[None, :] // DV) * sign).astype(BF16)
    expand = jnp.where(col[:, None] < 2 * heads, expand, 0).astype(BF16)

    def k_map(r):
        return lambda b, s, pt: (pt[b, jnp.minimum(s, nk - 1) * pp + r], 0, 0)

    def v_map(r):
        return lambda b, s, pt: (pt[b, jnp.maximum(s - nk, 0) * pp + r], 0, 0)

    row3 = lambda b, s, pt: (b, 0, 0)
    const2 = lambda b, s, pt: (0, 0)
    in_specs = [pl.BlockSpec((4, DQK), const2),
                pl.BlockSpec((1, 1, width), row3),
                pl.BlockSpec((1, 1, width), row3),
                pl.BlockSpec((1, 1, width), row3),
                pl.BlockSpec((past, LANES), const2),
                pl.BlockSpec((1, LANES), const2),
                pl.BlockSpec((width, LANES), const2),
                pl.BlockSpec((LANES, width), const2)]
    in_specs += [pl.BlockSpec((1, PAGE_SIZE, width), k_map(r)) for r in range(pp)]
    in_specs += [pl.BlockSpec((1, PAGE_SIZE, width), v_map(r)) for r in range(pp)]
    body = functools.partial(_attn_decode_body, n_pages=n_pages, lam_init=lam_init)
    out = pl.pallas_call(
        body,
        grid_spec=pltpu.PrefetchScalarGridSpec(
            num_scalar_prefetch=1,
            grid=(bd, 2 * nk),
            in_specs=in_specs,
            out_specs=pl.BlockSpec((1, 1, width), row3),
            scratch_shapes=[pltpu.VMEM((past, LANES), F32), pltpu.VMEM((8, LANES), F32),
                            pltpu.VMEM((8, width), F32)]),
        out_shape=jax.ShapeDtypeStruct((bd, 1, width), F32),
        compiler_params=_params("parallel", "arbitrary"),
        name="attn_decode",
    )(page_table, lam_vecs, q.reshape(bd, 1, width), k_new.reshape(bd, 1, width),
      v_new.reshape(bd, 1, width), bias_past, bias_new, sel, expand,
      *([ck] * pp), *([cv] * pp))
    return out.reshape(bd, width)


def _subln_body(x_ref, g_ref, o_ref, *, lam_init):
    g = g_ref[...]
    for h in range(x_ref.shape[1] // DV):
        x = x_ref[:, h * DV:(h + 1) * DV]
        ms = jnp.mean(x * x, axis=-1, keepdims=True)
        y = x * lax.rsqrt(ms + EPS) * g * (1.0 - lam_init)
        o_ref[:, h * DV:(h + 1) * DV] = y.astype(o_ref.dtype)


def _subln(x, g, lam_init):
    return pl.pallas_call(
        functools.partial(_subln_body, lam_init=lam_init),
        out_shape=jax.ShapeDtypeStruct(x.shape, BF16),
        name="subln",
    )(x, g.reshape(1, DV))


def _topk_rows(s, k):
    rows = s.shape[0]
    iota = lax.broadcasted_iota(jnp.int32, s.shape, 0)
    rank = jnp.full(s.shape, k, jnp.int32)
    vals = []
    for p in range(k):
        m = jnp.max(s, axis=0, keepdims=True)
        idx = jnp.min(jnp.where(s == m, iota, rows), axis=0, keepdims=True)
        sel = iota == idx
        rank = jnp.where(sel, p, rank)
        s = jnp.where(sel, -jnp.inf, s)
        vals.append(m)
    return vals, rank


def _route_body(qr_ref, keys_ref, n1_ref, w1_ref, r2_ref, w2_ref):
    k = PEER_TOPK
    nt = (((1,), (1,)), ((), ()))
    s1 = lax.dot_general(keys_ref[0], qr_ref[:, 0:LANES], nt, preferred_element_type=F32)
    s2 = lax.dot_general(keys_ref[1], qr_ref[:, LANES:2 * LANES], nt, preferred_element_type=F32)
    v1, rank1 = _topk_rows(s1, k)
    v2, rank2 = _topk_rows(s2, k)
    v2_all = jnp.concatenate(v2, axis=0)
    cand = jnp.concatenate([v1[p] + v2_all for p in range(k)], axis=0)
    top, rank_c = _topk_rows(cand, k)
    chosen = (rank_c < k).astype(F32)
    z = jnp.zeros_like(top[0])
    for i in range(k):
        z = z + jnp.exp(top[i] - top[0])
    n1 = jnp.zeros(s1.shape, F32)
    for p in range(k):
        n_p = jnp.sum(chosen[p * k:(p + 1) * k], axis=0, keepdims=True)
        n1 = jnp.where(rank1 == p, n_p, n1)
    n1_ref[0] = n1
    w1_ref[0] = jnp.exp(s1 - v1[0]) / z
    r2_ref[0] = rank2.astype(F32)
    w2_ref[0] = jnp.exp(s2 - v2[0])


def _route(qr, sub_keys, tt):
    n = qr.shape[0]
    heads, _, n_keys, d_half = sub_keys.shape
    keys = sub_keys.reshape(heads * 2, n_keys, d_half).astype(BF16)
    out = jax.ShapeDtypeStruct((heads, n_keys, n), F32)
    ospec = pl.BlockSpec((1, n_keys, tt), lambda i, h: (h, 0, i))
    return pl.pallas_call(
        _route_body,
        grid=(n // tt, heads),
        in_specs=[pl.BlockSpec((tt, 2 * d_half), lambda i, h: (i, h)),
                  pl.BlockSpec((2, n_keys, d_half), lambda i, h: (h, 0, 0))],
        out_specs=[ospec, ospec, ospec, ospec],
        out_shape=[out, out, out, out],
        compiler_params=_params("parallel", "arbitrary"),
        name="peer_route",
    )(qr, keys)


def _gelu(x):
    return 0.5 * x * (1.0 + lax.erf(x * (2.0 ** -0.5)))


def _peer_body(h_ref, u_ref, v_ref, n1_ref, w1_ref, r2_ref, w2_ref, x_ref, o_ref,
               z_sc, a_sc, *, ec, tt, heads, n_keys):
    j = pl.program_id(1)

    @pl.when(j == 0)
    def _init():
        o_ref[...] = x_ref[...]

    z_sc[...] = lax.dot_general(u_ref[...], h_ref[...], (((1,), (1,)), ((), ())),
                                preferred_element_type=F32)
    sub = lax.broadcasted_iota(jnp.int32, (8, LANES), 0)
    for sl in range(ec // n_keys):
        e1 = j * (ec // n_keys) + sl
        grp = pl.multiple_of((e1 // 8) * 8, 8)
        pick = sub == e1 % 8
        for tb in range(tt // LANES):
            cols = slice(tb * LANES, (tb + 1) * LANES)
            g = jnp.zeros((n_keys, LANES), F32)
            for h in range(heads):
                n1 = jnp.sum(jnp.where(pick, n1_ref[h, pl.ds(grp, 8), cols], 0.0),
                             axis=0, keepdims=True)
                w1 = jnp.sum(jnp.where(pick, w1_ref[h, pl.ds(grp, 8), cols], 0.0),
                             axis=0, keepdims=True)
                g = g + jnp.where(r2_ref[h, :, cols] < n1, w2_ref[h, :, cols] * w1, 0.0)
            z = z_sc[sl * n_keys:(sl + 1) * n_keys, cols]
            a = _gelu(z) * g
            a_sc[sl * n_keys:(sl + 1) * n_keys, cols] = a.astype(a_sc.dtype)
    o_ref[...] += lax.dot_general(a_sc[...], v_ref[...], (((0,), (0,)), ((), ())),
                                  preferred_element_type=F32)


def _peer(h2, u_bf, v_bf, tables, x, tt, ec):
    n, d = h2.shape
    n_exp = u_bf.shape[0]
    heads, n_keys, _ = tables[0].shape
    once = pl.Buffered(1)
    tspec = pl.BlockSpec((heads, n_keys, tt), lambda i, j: (0, 0, i), pipeline_mode=once)
    body = functools.partial(_peer_body, ec=ec, tt=tt, heads=heads, n_keys=n_keys)
    return pl.pallas_call(
        body,
        grid=(n // tt, n_exp // ec),
        in_specs=[pl.BlockSpec((tt, d), lambda i, j: (i, 0), pipeline_mode=once),
                  pl.BlockSpec((ec, d), lambda i, j: (j, 0)),
                  pl.BlockSpec((ec, d), lambda i, j: (j, 0)),
                  tspec, tspec, tspec, tspec,
                  pl.BlockSpec((tt, d), lambda i, j: (i, 0), pipeline_mode=once)],
        out_specs=pl.BlockSpec((tt, d), lambda i, j: (i, 0)),
        out_shape=jax.ShapeDtypeStruct((n, d), F32),
        scratch_shapes=[pltpu.VMEM((ec, tt), F32), pltpu.VMEM((ec, tt), BF16)],
        compiler_params=_params("parallel", "arbitrary"),
        name="peer_experts",
    )(h2, u_bf, v_bf, *tables, x)


def _tile(n, pref):
    return pref if n % pref == 0 else n


def _project(x2d, ln_g, w_in_bf, qn_g, kn_g, widths):
    pool_w, attn_w = widths
    n, d = x2d.shape
    tm = _tile(n, 1024)
    xn = _rmsnorm(x2d, ln_g, _tile(n, 256))
    mm = functools.partial(_matmul, [xn], [(w_in_bf, 0, d)], tm=tm, tn=512, out_dtype=F32)
    u = mm(n_out=pool_w, w_col_off=0, name="proj_u")
    q = mm(n_out=attn_w, w_col_off=pool_w, gain=qn_g, qk_group=DQK, name="proj_q")
    k = mm(n_out=attn_w, w_col_off=pool_w + attn_w, gain=kn_g, qk_group=DQK, name="proj_k")
    v = mm(n_out=attn_w, w_col_off=pool_w + 2 * attn_w, name="proj_v")
    return u, q, k, v


def _mix_and_peer(x2d, pool_y, att, w_o_bf, ln2_g, w_query_bf, sub_keys, u_bf, v_bf):
    n, d = x2d.shape
    pool_w = pool_y.shape[1]
    tm = _tile(n, 1024)
    assert att.shape[1] == pool_w, "w_o row blocks assume equal pool and attention widths"
    x1 = _matmul([pool_y, att], [(w_o_bf, 0, pool_w), (w_o_bf, 1, pool_w)],
                 n_out=d, tm=tm, tn=512, out_dtype=F32, res=x2d, name="out_proj")
    h2 = _rmsnorm(x1, ln2_g, _tile(n, 256))
    qr = _matmul([h2], [(w_query_bf, 0, d)], n_out=w_query_bf.shape[1], tm=tm, tn=512,
                 out_dtype=BF16, name="peer_query")
    tables = _route(qr, sub_keys, _tile(n, LANES))
    return _peer(h2, u_bf, v_bf, tables, x1, _tile(n, 512), 256)


def kernel(x_prompt, x_sample, cache_k, cache_v, state_pool, page_table, rel_bias, ln1_g, w_in, q_norm_g, k_norm_g, lambda_q1, lambda_k1, lambda_q2, lambda_k2, subln_g, w_pool, pool_scale, w_o, ln2_g, w_query, sub_keys, peer_u, peer_v):
    batch, seq, d = x_prompt.shape
    bd, tq, _ = x_sample.shape
    assert tq == 1, "sample group is one new position per sequence"
    depth = w_in.shape[0]
    pool_w = w_pool.shape[1] * w_pool.shape[2]
    attn_w = (w_in.shape[2] - pool_w) // 3
    heads = attn_w // DV
    n_state = state_pool.shape[2]

    xp = x_prompt.reshape(batch * seq, d)
    xs = x_sample.reshape(bd, d)
    kp_l, vp_l, pp_l, ks_l, vs_l, ps_l = [], [], [], [], [], []
    for l in range(depth):
        lam_init = 0.8 - 0.6 * math.exp(-0.3 * l)
        lam_vecs = jnp.stack([lambda_q1[l], lambda_k1[l], lambda_q2[l], lambda_k2[l]]).astype(F32)
        w_in_bf = w_in[l].astype(BF16)
        w_o_bf = w_o[l].astype(BF16)
        w_query_bf = w_query[l].astype(BF16)
        u_bf = peer_u[l].astype(BF16)
        v_bf = peer_v[l].astype(BF16)

        u, q, k, v = _project(xp, ln1_g[l], w_in_bf, q_norm_g[l], k_norm_g[l], (pool_w, attn_w))
        pool_y = _pool_prompt(u, w_pool[l], pool_scale[l], batch, seq)
        att = _attn_prompt(q, k, v, lam_vecs, rel_bias, subln_g[l], batch, seq, lam_init)
        xp = _mix_and_peer(xp, pool_y, att, w_o_bf, ln2_g[l], w_query_bf, sub_keys[l], u_bf, v_bf)
        kp_l.append(k.reshape(batch, seq, heads, 2 * DQK))
        vp_l.append(v.reshape(batch, seq, heads, DV))
        pp_l.append(u.reshape(batch, seq, pool_w)[:, seq - n_state:])

        u, q, k, v = _project(xs, ln1_g[l], w_in_bf, q_norm_g[l], k_norm_g[l], (pool_w, attn_w))
        pool_y = _pool_sample(state_pool[l], u, w_pool[l], pool_scale[l])
        att_raw = _attn_decode(q, k, v, cache_k[l], cache_v[l], page_table, lam_vecs, rel_bias,
                               lam_init)
        att = _subln(att_raw, subln_g[l], lam_init)
        xs = _mix_and_peer(xs, pool_y, att, w_o_bf, ln2_g[l], w_query_bf, sub_keys[l], u_bf, v_bf)
        ks_l.append(k.reshape(bd, tq, heads, 2 * DQK))
        vs_l.append(v.reshape(bd, tq, heads, DV))
        ps_l.append(jnp.concatenate([state_pool[l], u[:, None, :]], axis=1)[:, -n_state:])

    return (xp.reshape(batch, seq, d), xs.reshape(bd, tq, d),
            jnp.stack(kp_l), jnp.stack(vp_l), jnp.stack(pp_l),
            jnp.stack(ks_l), jnp.stack(vs_l), jnp.stack(ps_l))
```

```python
import functools
import math

import jax
import jax.numpy as jnp
from jax import lax
from jax.experimental import pallas as pl
from jax.experimental.pallas import tpu as pltpu

F32 = jnp.float32
BF16 = jnp.bfloat16
EPS = 1e-6

LANES = 128
VMEM_LIMIT_BYTES = 56 * 1024 * 1024

POOL_WINDOWS = (2, 4, 8, 16)
DQK = 64
DV = 128
NUM_BUCKETS = 32
MAX_DISTANCE = 128
PEER_TOPK = 16
PAGE_SIZE = 128


def _params(*sem):
    return pltpu.CompilerParams(dimension_semantics=sem, vmem_limit_bytes=VMEM_LIMIT_BYTES)


def _split_bf16(x):
    hi = x.astype(BF16)
    lo = (x - hi.astype(F32)).astype(BF16)
    return hi, lo


def _dot(a, b):
    return jnp.dot(a, b, preferred_element_type=F32)


def _dot2(x, w):
    hi, lo = _split_bf16(x)
    return _dot(hi, w) + _dot(lo, w)


def _rmsnorm_body(x_ref, g_ref, o_ref):
    x = x_ref[...]
    ms = jnp.mean(x * x, axis=-1, keepdims=True)
    o_ref[...] = (x * lax.rsqrt(ms + EPS) * g_ref[...]).astype(o_ref.dtype)


def _rmsnorm(x, g, tm):
    m, d = x.shape
    return pl.pallas_call(
        _rmsnorm_body,
        grid=(m // tm,),
        in_specs=[pl.BlockSpec((tm, d), lambda i: (i, 0)),
                  pl.BlockSpec((1, d), lambda i: (0, 0))],
        out_specs=pl.BlockSpec((tm, d), lambda i: (i, 0)),
        out_shape=jax.ShapeDtypeStruct((m, d), BF16),
        compiler_params=_params("parallel"),
        name="rmsnorm",
    )(x, g.reshape(1, d))


def _mm_body(*refs, n_pairs, qk_group, has_res):
    a_refs = refs[:n_pairs]
    w_refs = refs[n_pairs:2 * n_pairs]
    pos = 2 * n_pairs
    if qk_group:
        g_ref, ones_ref = refs[pos], refs[pos + 1]
        pos += 2
    if has_res:
        r_ref = refs[pos]
        pos += 1
    o_ref = refs[pos]

    acc = _dot(a_refs[0][...], w_refs[0][...])
    for p in range(1, n_pairs):
        acc = acc + _dot(a_refs[p][...], w_refs[p][...])
    if qk_group:
        sq = acc * acc
        hi, lo = _split_bf16(sq)
        ones = ones_ref[...]
        parts = []
        for s in range(acc.shape[1] // LANES):
            sl = slice(s * LANES, (s + 1) * LANES)
            parts.append(_dot(hi[:, sl], ones) + _dot(lo[:, sl], ones))
        gs = jnp.concatenate(parts, axis=1)
        acc = acc * lax.rsqrt(gs * (1.0 / qk_group) + EPS) * g_ref[...]
    if has_res:
        acc = acc + r_ref[...]
    o_ref[...] = acc.astype(o_ref.dtype)


def _matmul(a_list, w_list, *, n_out, tm, tn, out_dtype, w_col_off=0,
            gain=None, qk_group=0, res=None, name="matmul"):
    m = a_list[0].shape[0]
    tn = _tile(n_out, tn)
    off = w_col_off // tn
    in_specs, args = [], []
    for a in a_list:
        in_specs.append(pl.BlockSpec((tm, a.shape[1]), lambda i, j: (i, 0)))
        args.append(a)
    for w, rb, kr in w_list:
        in_specs.append(pl.BlockSpec((kr, tn), lambda i, j, rb=rb: (rb, j + off)))
        args.append(w)
    if qk_group:
        lane = jnp.arange(LANES)
        ones_bd = (lane[:, None] // qk_group == lane[None, :] // qk_group).astype(BF16)
        in_specs += [pl.BlockSpec((1, tn), lambda i, j: (0, 0)),
                     pl.BlockSpec((LANES, LANES), lambda i, j: (0, 0))]
        args += [jnp.tile(gain.astype(F32), tn // gain.shape[0]).reshape(1, tn), ones_bd]
    if res is not None:
        in_specs.append(pl.BlockSpec((tm, tn), lambda i, j: (i, j)))
        args.append(res)
    body = functools.partial(_mm_body, n_pairs=len(a_list), qk_group=qk_group,
                             has_res=res is not None)
    return pl.pallas_call(
        body,
        grid=(m // tm, n_out // tn),
        in_specs=in_specs,
        out_specs=pl.BlockSpec((tm, tn), lambda i, j: (i, j)),
        out_shape=jax.ShapeDtypeStruct((m, n_out), out_dtype),
        compiler_params=_params("parallel", "arbitrary"),
        name=name,
    )(*args)


_POOL_PAD = 16
_POOL_ROWS = 256


def _pool_prompt_body(u_ref, w_ref, sc_ref, o_ref, pad_ref, *, seq):
    g = pl.program_id(1)
    pad_ref[0:_POOL_PAD, :] = jnp.zeros((_POOL_PAD, pad_ref.shape[1]), F32)
    pad_ref[_POOL_PAD:, :] = u_ref[...]
    wmat = w_ref[0].astype(BF16)
    scale = sc_ref[...]
    for gi, win in enumerate(POOL_WINDOWS):
        @pl.when(g == gi)
        def _(win=win):
            for r0 in range(0, seq, _POOL_ROWS):
                cur = pad_ref[r0 + _POOL_PAD:r0 + _POOL_PAD + _POOL_ROWS, :]
                tot = cur
                for i in range(1, win):
                    tot = tot + pad_ref[r0 + _POOL_PAD - i:r0 + _POOL_PAD - i + _POOL_ROWS, :]
                pos = r0 + lax.broadcasted_iota(jnp.int32, (_POOL_ROWS, 1), 0)
                cnt = jnp.minimum(win, pos + 1).astype(F32)
                d = tot / cnt - cur
                y = _dot(d.astype(BF16), wmat) * scale
                o_ref[r0:r0 + _POOL_ROWS, :] = y.astype(o_ref.dtype)


def _pool_prompt(u, w_pool, pool_scale, batch, seq):
    n, width = u.shape
    ng = len(POOL_WINDOWS)
    gw = width // ng
    body = functools.partial(_pool_prompt_body, seq=seq)
    return pl.pallas_call(
        body,
        grid=(batch, ng),
        in_specs=[pl.BlockSpec((seq, gw), lambda b, g: (b, g)),
                  pl.BlockSpec((1, gw, gw), lambda b, g: (g, 0, 0)),
                  pl.BlockSpec((1, gw), lambda b, g: (0, g))],
        out_specs=pl.BlockSpec((seq, gw), lambda b, g: (b, g)),
        out_shape=jax.ShapeDtypeStruct((n, width), BF16),
        scratch_shapes=[pltpu.VMEM((seq + _POOL_PAD, gw), F32)],
        compiler_params=_params("parallel", "arbitrary"),
        name="pool_prompt",
    )(u, w_pool, pool_scale.reshape(1, width))


def _pool_sample_body(st_ref, u_ref, w_ref, sc_ref, o_ref, *, n_state, width):
    ng = len(POOL_WINDOWS)
    gw = width // ng
    for gi, win in enumerate(POOL_WINDOWS):
        c0 = gi * gw
        cur = u_ref[:, c0:c0 + gw]
        tot = cur
        for i in range(1, win):
            r = n_state - i
            tot = tot + st_ref[:, r * width + c0:r * width + c0 + gw]
        d = tot / float(win) - cur
        y = _dot(d.astype(BF16), w_ref[gi].astype(BF16)) * sc_ref[:, c0:c0 + gw]
        o_ref[:, c0:c0 + gw] = y.astype(o_ref.dtype)


def _pool_sample(state, u, w_pool, pool_scale):
    bd, n_state, width = state.shape
    body = functools.partial(_pool_sample_body, n_state=n_state, width=width)
    return pl.pallas_call(
        body,
        out_shape=jax.ShapeDtypeStruct((bd, width), BF16),
        compiler_params=pltpu.CompilerParams(vmem_limit_bytes=VMEM_LIMIT_BYTES),
        name="pool_sample",
    )(state.reshape(bd, n_state * width), u, w_pool, pool_scale.reshape(1, width))


def _rel_bucket(n):
    max_exact = NUM_BUCKETS // 2
    nf = jnp.maximum(n, 1).astype(F32)
    large = max_exact + (jnp.log(nf / max_exact) / math.log(MAX_DISTANCE / max_exact)
                         * (NUM_BUCKETS - max_exact)).astype(jnp.int32)
    large = jnp.minimum(large, NUM_BUCKETS - 1)
    return jnp.where(n < max_exact, n, large)


def _bias_by_distance(rel_bias, n_dist):
    return rel_bias.astype(F32)[_rel_bucket(jnp.arange(n_dist))]


def _lambda(lv_ref, lam_init):
    lv = lv_ref[...]
    a = jnp.sum(lv[0:1] * lv[1:2], axis=-1, keepdims=True)
    b = jnp.sum(lv[2:3] * lv[3:4], axis=-1, keepdims=True)
    return jnp.exp(a) - jnp.exp(b) + lam_init


def _attn_prompt_body(lv_ref, q_ref, k_ref, v_ref, bias_ref, g_ref, o_ref,
                      vt_sc, m_sc, l_sc, acc_sc, *, tq, tk, seq, lam_init):
    qi = pl.program_id(2)

    @pl.when(qi == 0)
    def _transpose_values():
        for c in range(seq // tk):
            vt_sc[c] = v_ref[c * tk:(c + 1) * tk, :].T.astype(BF16)

    q = q_ref[...] * (DQK ** -0.5)
    lane = lax.broadcasted_iota(jnp.int32, q.shape, 1)
    q_ext = jnp.concatenate([jnp.where(lane < DQK, q, 0.0),
                             jnp.where(lane >= DQK, q, 0.0)], axis=0).astype(BF16)
    m_sc[...] = jnp.full(m_sc.shape, -jnp.inf, F32)
    l_sc[...] = jnp.zeros(l_sc.shape, F32)
    acc_sc[...] = jnp.zeros(acc_sc.shape, F32)

    def step(j, carry):
        r0 = pl.multiple_of(j * tk, tk)
        kj = k_ref[pl.ds(r0, tk), :].astype(BF16)
        st = lax.dot_general(kj, q_ext, (((1,), (1,)), ((), ())),
                             preferred_element_type=F32)
        bt = bias_ref[0, jnp.minimum(qi - j, 2)]
        st = st + jnp.concatenate([bt, bt], axis=1)
        m_old = m_sc[...]
        m_new = jnp.maximum(m_old, jnp.max(st, axis=0, keepdims=True))
        alpha = jnp.exp(m_old - m_new)
        p = jnp.exp(st - m_new)
        l_sc[...] = alpha * l_sc[...] + jnp.sum(p, axis=0, keepdims=True)
        acc_sc[...] = alpha * acc_sc[...] + _dot(vt_sc[j], p.astype(BF16))
        m_sc[...] = m_new
        return carry

    lax.fori_loop(0, qi + 1, step, 0)

    lam = _lambda(lv_ref, lam_init)
    o = acc_sc[...] / l_sc[...]
    att = (o[:, :tq] - lam * o[:, tq:]).T
    ms = jnp.mean(att * att, axis=-1, keepdims=True)
    out = att * lax.rsqrt(ms + EPS) * g_ref[...] * (1.0 - lam_init)
    o_ref[...] = out.astype(o_ref.dtype)


def _toeplitz(g, n):
    h = g.shape[0]
    gp = jnp.pad(g, ((0, 0), (0, 1)))
    flat = jnp.tile(gp, (1, n))[:, :n * (2 * n - 1)]
    return flat.reshape(h, n, 2 * n - 1)[:, :, n - 1:]


def _attn_prompt(q, k, v, lam_vecs, rel_bias, subln_g, batch, seq, lam_init, tq=256):
    n, width = q.shape
    heads = width // DV
    tk = tq
    nq = seq // tq
    assert tq == tk and tq >= MAX_DISTANCE
    bd = _bias_by_distance(rel_bias, 2 * tq).T
    masked = jnp.full((heads, tq - 1), -jnp.inf, F32)
    tile0 = _toeplitz(jnp.concatenate([masked, bd[:, :tq]], axis=1), tq)
    tile1 = _toeplitz(bd[:, 1:2 * tq], tq)
    tile2 = jnp.broadcast_to(bd[:, 2 * tq - 1][:, None, None], (heads, tk, tq))
    bias_tiles = jnp.stack([tile0, tile1, tile2], axis=1)
    body = functools.partial(_attn_prompt_body, tq=tq, tk=tk, seq=seq, lam_init=lam_init)
    return pl.pallas_call(
        body,
        grid=(batch, heads, nq),
        in_specs=[pl.BlockSpec((4, DQK), lambda b, h, i: (0, 0)),
                  pl.BlockSpec((tq, DV), lambda b, h, i: (b * nq + i, h)),
                  pl.BlockSpec((seq, DV), lambda b, h, i: (b, h)),
                  pl.BlockSpec((seq, DV), lambda b, h, i: (b, h)),
                  pl.BlockSpec((1, 3, tk, tq), lambda b, h, i: (h, 0, 0, 0)),
                  pl.BlockSpec((1, DV), lambda b, h, i: (0, 0))],
        out_specs=pl.BlockSpec((tq, DV), lambda b, h, i: (b * nq + i, h)),
        out_shape=jax.ShapeDtypeStruct((n, width), BF16),
        scratch_shapes=[pltpu.VMEM((seq // tk, DV, tk), BF16),
                        pltpu.VMEM((1, 2 * tq), F32), pltpu.VMEM((1, 2 * tq), F32),
                        pltpu.VMEM((DV, 2 * tq), F32)],
        compiler_params=_params("parallel", "parallel", "arbitrary"),
        name="attn_prompt",
    )(lam_vecs, q, k, v, bias_tiles, subln_g.reshape(1, DV))


_PAGES_PER_STEP = 4


def _attn_decode_body(pt_ref, lv_ref, q_ref, kn_ref, vn_ref, bias_ref, bias0_ref, g_ref,
                      *rest, n_pages, heads, lam_init):
    pp = _PAGES_PER_STEP
    k_refs = rest[:pp]
    v_refs = rest[pp:2 * pp]
    o_ref = rest[2 * pp]
    p_sc, m_sc, l_sc, f_sc, wn_sc, acc_sc = rest[2 * pp + 1:]
    del pt_ref
    nk = n_pages // pp
    s = pl.program_id(1)
    q = q_ref[0] * (DQK ** -0.5)
    lane = lax.broadcasted_iota(jnp.int32, q.shape, 1)
    q_all = jnp.concatenate([jnp.where(lane < DQK, q, 0.0),
                             jnp.where(lane >= DQK, q, 0.0)], axis=0)
    stat = m_sc.shape[1:]

    @pl.when(s < nk)
    def _scores():
        qb = q_all.astype(BF16)
        for r in range(pp):
            pg = s * pp + r
            sc = lax.dot_general(qb, k_refs[r][0].astype(BF16), (((1,), (1,)), ((), ())),
                                 preferred_element_type=F32)
            sc = sc + bias_ref[pg]
            m_pg = jnp.max(sc, axis=1, keepdims=True)
            p = jnp.exp(sc - m_pg)
            p_sc[pg] = p
            m_sc[pg] = jnp.broadcast_to(m_pg, stat)
            l_sc[pg] = jnp.broadcast_to(jnp.sum(p, axis=1, keepdims=True), stat)

    @pl.when(s == nk - 1)
    def _merge():
        kn = kn_ref[0]
        sn = (jnp.sum(q_all * jnp.concatenate([kn, kn], axis=0), axis=1, keepdims=True)
              + bias0_ref[...])
        m_all = m_sc[...]
        m = jnp.maximum(jnp.max(m_all, axis=0), sn)
        e = jnp.exp(m_all - m[None])
        pn = jnp.exp(sn - m)
        l = jnp.sum(l_sc[...] * e, axis=0) + pn
        lam = _lambda(lv_ref, lam_init)
        r = jnp.concatenate([1.0 / l[:heads], lam / l[heads:]], axis=0)
        f_sc[...] = e * r[None]
        wn = pn * r
        wn_sc[...] = wn[:heads] - wn[heads:]
        acc_sc[...] = jnp.zeros(acc_sc.shape, F32)

    @pl.when(s >= nk)
    def _values():
        tot = acc_sc[...]
        reps = p_sc.shape[2] // LANES
        for r in range(pp):
            pg = (s - nk) * pp + r
            pw = p_sc[pg] * jnp.tile(f_sc[pg], (1, reps))
            w = (pw[:heads] - pw[heads:]).astype(BF16)
            tot = tot + _dot(w, v_refs[r][0].astype(BF16))
        acc_sc[...] = tot

    @pl.when(s == 2 * nk - 1)
    def _finish():
        o = acc_sc[...] + wn_sc[...] * vn_ref[0]
        ms = jnp.mean(o * o, axis=-1, keepdims=True)
        o_ref[0] = o * lax.rsqrt(ms + EPS) * g_ref[...] * (1.0 - lam_init)


def _attn_decode(q, k_new, v_new, cache_k, cache_v, layer, page_table, lam_vecs, rel_bias,
                 subln_g, lam_init):
    bd, width = q.shape
    heads = width // DV
    n_pages = page_table.shape[1]
    past = n_pages * PAGE_SIZE
    pp = _PAGES_PER_STEP
    nk = n_pages // pp
    depth, n_pool = cache_k.shape[:2]
    ck = cache_k.reshape(depth * n_pool, PAGE_SIZE * heads, DV)
    cv = cache_v.reshape(depth * n_pool, PAGE_SIZE * heads, DV)
    page0 = layer * n_pool
    bdist = _bias_by_distance(rel_bias, past + 1)
    b_past = bdist[1:][::-1].reshape(n_pages, 1, PAGE_SIZE, heads)
    own_head = jnp.eye(heads, dtype=bool)[None, :, None, :]
    b_past = jnp.where(own_head, b_past, -jnp.inf).reshape(n_pages, heads, PAGE_SIZE * heads)
    b_past = jnp.concatenate([b_past, b_past], axis=1)
    b_new = jnp.broadcast_to(jnp.tile(bdist[0], 2)[:, None], (2 * heads, LANES))

    def k_map(r):
        return lambda b, s, pt: (page0 + pt[b, jnp.minimum(s, nk - 1) * pp + r], 0, 0)

    def v_map(r):
        return lambda b, s, pt: (page0 + pt[b, jnp.maximum(s - nk, 0) * pp + r], 0, 0)

    row3 = lambda b, s, pt: (b, 0, 0)
    const2 = lambda b, s, pt: (0, 0)
    page_rows = PAGE_SIZE * heads
    in_specs = [pl.BlockSpec((4, DQK), const2),
                pl.BlockSpec((1, heads, DV), row3),
                pl.BlockSpec((1, heads, DV), row3),
                pl.BlockSpec((1, heads, DV), row3),
                pl.BlockSpec((n_pages, 2 * heads, page_rows), lambda b, s, pt: (0, 0, 0),
                             pipeline_mode=pl.Buffered(1)),
                pl.BlockSpec((2 * heads, LANES), const2),
                pl.BlockSpec((1, DV), const2)]
    in_specs += [pl.BlockSpec((1, page_rows, DV), k_map(r)) for r in range(pp)]
    in_specs += [pl.BlockSpec((1, page_rows, DV), v_map(r)) for r in range(pp)]
    body = functools.partial(_attn_decode_body, n_pages=n_pages, heads=heads, lam_init=lam_init)
    out = pl.pallas_call(
        body,
        grid_spec=pltpu.PrefetchScalarGridSpec(
            num_scalar_prefetch=1,
            grid=(bd, 2 * nk),
            in_specs=in_specs,
            out_specs=pl.BlockSpec((1, heads, DV), row3),
            scratch_shapes=[pltpu.VMEM((n_pages, 2 * heads, page_rows), F32),
                            pltpu.VMEM((n_pages, 2 * heads, LANES), F32),
                            pltpu.VMEM((n_pages, 2 * heads, LANES), F32),
                            pltpu.VMEM((n_pages, 2 * heads, LANES), F32),
                            pltpu.VMEM((heads, LANES), F32),
                            pltpu.VMEM((heads, DV), F32)]),
        out_shape=jax.ShapeDtypeStruct((bd, heads, DV), F32),
        compiler_params=_params("parallel", "arbitrary"),
        name="attn_decode",
    )(page_table, lam_vecs, q.reshape(bd, heads, DV), k_new.reshape(bd, heads, DV),
      v_new.reshape(bd, heads, DV), b_past, b_new, subln_g.reshape(1, DV),
      *([ck] * pp), *([cv] * pp))
    return out.reshape(bd, width)


def _topk_rows(s, k):
    rows = s.shape[0]
    iota = lax.broadcasted_iota(jnp.int32, s.shape, 0)
    rank = jnp.full(s.shape, k, jnp.int32)
    vals = []
    for p in range(k):
        m = jnp.max(s, axis=0, keepdims=True)
        idx = jnp.min(jnp.where(s == m, iota, rows), axis=0, keepdims=True)
        sel = iota == idx
        rank = jnp.where(sel, p, rank)
        s = jnp.where(sel, -jnp.inf, s)
        vals.append(m)
    return vals, rank


def _route_body(qr_ref, keys_ref, n1_ref, w1_ref, r2_ref, w2_ref):
    k = PEER_TOPK
    nt = (((1,), (1,)), ((), ()))
    s1 = lax.dot_general(keys_ref[0], qr_ref[:, 0:LANES], nt, preferred_element_type=F32)
    s2 = lax.dot_general(keys_ref[1], qr_ref[:, LANES:2 * LANES], nt, preferred_element_type=F32)
    v1, rank1 = _topk_rows(s1, k)
    v2, rank2 = _topk_rows(s2, k)
    v2_all = jnp.concatenate(v2, axis=0)
    cand = jnp.concatenate([v1[p] + v2_all for p in range(k)], axis=0)
    top, rank_c = _topk_rows(cand, k)
    chosen = (rank_c < k).astype(F32)
    z = jnp.zeros_like(top[0])
    for i in range(k):
        z = z + jnp.exp(top[i] - top[0])
    n1 = jnp.zeros(s1.shape, F32)
    for p in range(k):
        n_p = jnp.sum(chosen[p * k:(p + 1) * k], axis=0, keepdims=True)
        n1 = jnp.where(rank1 == p, n_p, n1)
    n1_ref[0] = n1
    w1_ref[0] = jnp.exp(s1 - v1[0]) / z
    r2_ref[0] = rank2.astype(F32)
    w2_ref[0] = jnp.exp(s2 - v2[0])


def _route(qr, sub_keys, tt):
    n = qr.shape[0]
    heads, _, n_keys, d_half = sub_keys.shape
    keys = sub_keys.reshape(heads * 2, n_keys, d_half).astype(BF16)
    out = jax.ShapeDtypeStruct((heads, n_keys, n), F32)
    ospec = pl.BlockSpec((1, n_keys, tt), lambda i, h: (h, 0, i))
    return pl.pallas_call(
        _route_body,
        grid=(n // tt, heads),
        in_specs=[pl.BlockSpec((tt, 2 * d_half), lambda i, h: (i, h)),
                  pl.BlockSpec((2, n_keys, d_half), lambda i, h: (h, 0, 0))],
        out_specs=[ospec, ospec, ospec, ospec],
        out_shape=[out, out, out, out],
        compiler_params=_params("parallel", "arbitrary"),
        name="peer_route",
    )(qr, keys)


def _gelu(x):
    return 0.5 * x * (1.0 + lax.erf(x * (2.0 ** -0.5)))


def _peer_body(h_ref, u_ref, v_ref, n1_ref, w1_ref, r2_ref, w2_ref, x_ref, o_ref,
               z_sc, a_sc, *, ec, tt, heads, n_keys):
    j = pl.program_id(1)

    @pl.when(j == 0)
    def _init():
        o_ref[...] = x_ref[...]

    z_sc[...] = lax.dot_general(u_ref[...], h_ref[...], (((1,), (1,)), ((), ())),
                                preferred_element_type=F32)
    sub = lax.broadcasted_iota(jnp.int32, (8, LANES), 0)
    for sl in range(ec // n_keys):
        e1 = j * (ec // n_keys) + sl
        grp = pl.multiple_of((e1 // 8) * 8, 8)
        pick = sub == e1 % 8
        for tb in range(tt // LANES):
            cols = slice(tb * LANES, (tb + 1) * LANES)
            g = jnp.zeros((n_keys, LANES), F32)
            for h in range(heads):
                n1 = jnp.sum(jnp.where(pick, n1_ref[h, pl.ds(grp, 8), cols], 0.0),
                             axis=0, keepdims=True)
                w1 = jnp.sum(jnp.where(pick, w1_ref[h, pl.ds(grp, 8), cols], 0.0),
                             axis=0, keepdims=True)
                g = g + jnp.where(r2_ref[h, :, cols] < n1, w2_ref[h, :, cols] * w1, 0.0)
            z = z_sc[sl * n_keys:(sl + 1) * n_keys, cols]
            a = _gelu(z) * g
            a_sc[sl * n_keys:(sl + 1) * n_keys, cols] = a.astype(a_sc.dtype)
    o_ref[...] += lax.dot_general(a_sc[...], v_ref[...], (((0,), (0,)), ((), ())),
                                  preferred_element_type=F32)


def _peer(h2, u_bf, v_bf, tables, x, tt, ec):
    n, d = h2.shape
    n_exp = u_bf.shape[0]
    heads, n_keys, _ = tables[0].shape
    once = pl.Buffered(1)
    tspec = pl.BlockSpec((heads, n_keys, tt), lambda i, j: (0, 0, i), pipeline_mode=once)
    body = functools.partial(_peer_body, ec=ec, tt=tt, heads=heads, n_keys=n_keys)
    return pl.pallas_call(
        body,
        grid=(n // tt, n_exp // ec),
        in_specs=[pl.BlockSpec((tt, d), lambda i, j: (i, 0), pipeline_mode=once),
                  pl.BlockSpec((ec, d), lambda i, j: (j, 0)),
                  pl.BlockSpec((ec, d), lambda i, j: (j, 0)),
                  tspec, tspec, tspec, tspec,
                  pl.BlockSpec((tt, d), lambda i, j: (i, 0), pipeline_mode=once)],
        out_specs=pl.BlockSpec((tt, d), lambda i, j: (i, 0)),
        out_shape=jax.ShapeDtypeStruct((n, d), F32),
        scratch_shapes=[pltpu.VMEM((ec, tt), F32), pltpu.VMEM((ec, tt), BF16)],
        compiler_params=_params("parallel", "arbitrary"),
        name="peer_experts",
    )(h2, u_bf, v_bf, *tables, x)


def _tile(n, pref):
    return pref if n % pref == 0 else n


def _project(x2d, ln_g, w_in_bf, qn_g, kn_g, widths):
    pool_w, attn_w = widths
    n, d = x2d.shape
    tm = _tile(n, 1024)
    xn = _rmsnorm(x2d, ln_g, _tile(n, 256))
    mm = functools.partial(_matmul, [xn], [(w_in_bf, 0, d)], tm=tm, tn=512, out_dtype=F32)
    u = mm(n_out=pool_w, w_col_off=0, name="proj_u")
    q = mm(n_out=attn_w, w_col_off=pool_w, gain=qn_g, qk_group=DQK, name="proj_q")
    k = mm(n_out=attn_w, w_col_off=pool_w + attn_w, gain=kn_g, qk_group=DQK, name="proj_k")
    v = mm(n_out=attn_w, w_col_off=pool_w + 2 * attn_w, name="proj_v")
    return u, q, k, v


def _mix_and_peer(x2d, pool_y, att, w_o_bf, ln2_g, w_query_bf, sub_keys, u_bf, v_bf):
    n, d = x2d.shape
    pool_w = pool_y.shape[1]
    tm = _tile(n, 1024)
    assert att.shape[1] == pool_w, "w_o row blocks assume equal pool and attention widths"
    x1 = _matmul([pool_y, att], [(w_o_bf, 0, pool_w), (w_o_bf, 1, pool_w)],
                 n_out=d, tm=tm, tn=512, out_dtype=F32, res=x2d, name="out_proj")
    h2 = _rmsnorm(x1, ln2_g, _tile(n, 256))
    qr = _matmul([h2], [(w_query_bf, 0, d)], n_out=w_query_bf.shape[1], tm=tm, tn=512,
                 out_dtype=BF16, name="peer_query")
    tables = _route(qr, sub_keys, _tile(n, LANES))
    return _peer(h2, u_bf, v_bf, tables, x1, _tile(n, 512), 256)


def kernel(x_prompt, x_sample, cache_k, cache_v, state_pool, page_table, rel_bias, ln1_g, w_in, q_norm_g, k_norm_g, lambda_q1, lambda_k1, lambda_q2, lambda_k2, subln_g, w_pool, pool_scale, w_o, ln2_g, w_query, sub_keys, peer_u, peer_v):
    batch, seq, d = x_prompt.shape
    bd, tq, _ = x_sample.shape
    assert tq == 1, "sample group is one new position per sequence"
    depth = w_in.shape[0]
    pool_w = w_pool.shape[1] * w_pool.shape[2]
    attn_w = (w_in.shape[2] - pool_w) // 3
    heads = attn_w // DV
    n_state = state_pool.shape[2]

    xp = x_prompt.reshape(batch * seq, d)
    xs = x_sample.reshape(bd, d)
    kp_l, vp_l, pp_l, ks_l, vs_l, ps_l = [], [], [], [], [], []
    for l in range(depth):
        lam_init = 0.8 - 0.6 * math.exp(-0.3 * l)
        lam_vecs = jnp.stack([lambda_q1[l], lambda_k1[l], lambda_q2[l], lambda_k2[l]]).astype(F32)
        w_in_bf = w_in[l].astype(BF16)
        w_o_bf = w_o[l].astype(BF16)
        w_query_bf = w_query[l].astype(BF16)
        u_bf = peer_u[l].astype(BF16)
        v_bf = peer_v[l].astype(BF16)

        u, q, k, v = _project(xp, ln1_g[l], w_in_bf, q_norm_g[l], k_norm_g[l], (pool_w, attn_w))
        pool_y = _pool_prompt(u, w_pool[l], pool_scale[l], batch, seq)
        att = _attn_prompt(q, k, v, lam_vecs, rel_bias, subln_g[l], batch, seq, lam_init)
        xp = _mix_and_peer(xp, pool_y, att, w_o_bf, ln2_g[l], w_query_bf, sub_keys[l], u_bf, v_bf)
        kp_l.append(k.reshape(batch, seq, heads, 2 * DQK))
        vp_l.append(v.reshape(batch, seq, heads, DV))
        pp_l.append(u.reshape(batch, seq, pool_w)[:, seq - n_state:])

        u, q, k, v = _project(xs, ln1_g[l], w_in_bf, q_norm_g[l], k_norm_g[l], (pool_w, attn_w))
        pool_y = _pool_sample(state_pool[l], u, w_pool[l], pool_scale[l])
        att = _attn_decode(q, k, v, cache_k, cache_v, l, page_table, lam_vecs, rel_bias,
                           subln_g[l], lam_init).astype(BF16)
        xs = _mix_and_peer(xs, pool_y, att, w_o_bf, ln2_g[l], w_query_bf, sub_keys[l], u_bf, v_bf)
        ks_l.append(k.reshape(bd, tq, heads, 2 * DQK))
        vs_l.append(v.reshape(bd, tq, heads, DV))
        ps_l.append(jnp.concatenate([state_pool[l], u[:, None, :]], axis=1)[:, -n_state:])

    return (xp.reshape(batch, seq, d), xs.reshape(bd, tq, d),
            jnp.stack(kp_l), jnp.stack(vp_l), jnp.stack(pp_l),
            jnp.stack(ks_l), jnp.stack(vs_l), jnp.stack(ps_l))
```

```python
import functools
import math

import jax
import jax.numpy as jnp
from jax import lax
from jax.experimental import pallas as pl
from jax.experimental.pallas import tpu as pltpu

F32 = jnp.float32
BF16 = jnp.bfloat16
EPS = 1e-6

LANES = 128
VMEM_LIMIT_BYTES = 56 * 1024 * 1024

POOL_WINDOWS = (2, 4, 8, 16)
DQK = 64
DV = 128
NUM_BUCKETS = 32
MAX_DISTANCE = 128
PEER_TOPK = 16
PAGE_SIZE = 128


def _params(*sem, flags=None):
    return pltpu.CompilerParams(dimension_semantics=sem, vmem_limit_bytes=VMEM_LIMIT_BYTES,
                                flags=flags)


def _split_bf16(x):
    hi = x.astype(BF16)
    lo = (x - hi.astype(F32)).astype(BF16)
    return hi, lo


def _dot(a, b):
    return jnp.dot(a, b, preferred_element_type=F32)


def _dot2(x, w):
    hi, lo = _split_bf16(x)
    return _dot(hi, w) + _dot(lo, w)


def _rmsnorm_body(x_ref, g_ref, o_ref):
    x = x_ref[...]
    ms = jnp.mean(x * x, axis=-1, keepdims=True)
    o_ref[...] = (x * lax.rsqrt(ms + EPS) * g_ref[...]).astype(o_ref.dtype)


def _rmsnorm(x, g, tm):
    m, d = x.shape
    return pl.pallas_call(
        _rmsnorm_body,
        grid=(m // tm,),
        in_specs=[pl.BlockSpec((tm, d), lambda i: (i, 0)),
                  pl.BlockSpec((1, d), lambda i: (0, 0))],
        out_specs=pl.BlockSpec((tm, d), lambda i: (i, 0)),
        out_shape=jax.ShapeDtypeStruct((m, d), BF16),
        compiler_params=_params("parallel"),
        name="rmsnorm",
    )(x, g.reshape(1, d))


def _mm_body(*refs, n_pairs, qk_group, has_res):
    a_refs = refs[:n_pairs]
    w_refs = refs[n_pairs:2 * n_pairs]
    pos = 2 * n_pairs
    if qk_group:
        g_ref, ones_ref = refs[pos], refs[pos + 1]
        pos += 2
    if has_res:
        r_ref = refs[pos]
        pos += 1
    o_ref = refs[pos]

    acc = _dot(a_refs[0][...], w_refs[0][...])
    for p in range(1, n_pairs):
        acc = acc + _dot(a_refs[p][...], w_refs[p][...])
    if qk_group:
        sq = acc * acc
        hi, lo = _split_bf16(sq)
        ones = ones_ref[...]
        parts = []
        for s in range(acc.shape[1] // LANES):
            sl = slice(s * LANES, (s + 1) * LANES)
            parts.append(_dot(hi[:, sl], ones) + _dot(lo[:, sl], ones))
        gs = jnp.concatenate(parts, axis=1)
        acc = acc * lax.rsqrt(gs * (1.0 / qk_group) + EPS) * g_ref[...]
    if has_res:
        acc = acc + r_ref[...]
    o_ref[...] = acc.astype(o_ref.dtype)


def _matmul(a_list, w_list, *, n_out, tm, tn, out_dtype, w_col_off=0,
            gain=None, qk_group=0, res=None, name="matmul"):
    m = a_list[0].shape[0]
    tn = _tile(n_out, tn)
    off = w_col_off // tn
    in_specs, args = [], []
    for a in a_list:
        in_specs.append(pl.BlockSpec((tm, a.shape[1]), lambda i, j: (i, 0)))
        args.append(a)
    for w, rb, kr in w_list:
        in_specs.append(pl.BlockSpec((kr, tn), lambda i, j, rb=rb: (rb, j + off)))
        args.append(w)
    if qk_group:
        lane = jnp.arange(LANES)
        ones_bd = (lane[:, None] // qk_group == lane[None, :] // qk_group).astype(BF16)
        in_specs += [pl.BlockSpec((1, tn), lambda i, j: (0, 0)),
                     pl.BlockSpec((LANES, LANES), lambda i, j: (0, 0))]
        args += [jnp.tile(gain.astype(F32), tn // gain.shape[0]).reshape(1, tn), ones_bd]
    if res is not None:
        in_specs.append(pl.BlockSpec((tm, tn), lambda i, j: (i, j)))
        args.append(res)
    body = functools.partial(_mm_body, n_pairs=len(a_list), qk_group=qk_group,
                             has_res=res is not None)
    return pl.pallas_call(
        body,
        grid=(m // tm, n_out // tn),
        in_specs=in_specs,
        out_specs=pl.BlockSpec((tm, tn), lambda i, j: (i, j)),
        out_shape=jax.ShapeDtypeStruct((m, n_out), out_dtype),
        compiler_params=_params("parallel", "arbitrary"),
        name=name,
    )(*args)


_POOL_PAD = 16
_POOL_ROWS = 256


def _pool_prompt_body(u_ref, w_ref, sc_ref, o_ref, pad_ref, *, seq):
    g = pl.program_id(1)
    pad_ref[0:_POOL_PAD, :] = jnp.zeros((_POOL_PAD, pad_ref.shape[1]), F32)
    pad_ref[_POOL_PAD:, :] = u_ref[...]
    wmat = w_ref[0].astype(BF16)
    scale = sc_ref[...]
    for gi, win in enumerate(POOL_WINDOWS):
        @pl.when(g == gi)
        def _(win=win):
            for r0 in range(0, seq, _POOL_ROWS):
                cur = pad_ref[r0 + _POOL_PAD:r0 + _POOL_PAD + _POOL_ROWS, :]
                tot = cur
                for i in range(1, win):
                    tot = tot + pad_ref[r0 + _POOL_PAD - i:r0 + _POOL_PAD - i + _POOL_ROWS, :]
                pos = r0 + lax.broadcasted_iota(jnp.int32, (_POOL_ROWS, 1), 0)
                cnt = jnp.minimum(win, pos + 1).astype(F32)
                d = tot / cnt - cur
                y = _dot(d.astype(BF16), wmat) * scale
                o_ref[r0:r0 + _POOL_ROWS, :] = y.astype(o_ref.dtype)


def _pool_prompt(u, w_pool, pool_scale, batch, seq):
    n, width = u.shape
    ng = len(POOL_WINDOWS)
    gw = width // ng
    body = functools.partial(_pool_prompt_body, seq=seq)
    return pl.pallas_call(
        body,
        grid=(batch, ng),
        in_specs=[pl.BlockSpec((seq, gw), lambda b, g: (b, g)),
                  pl.BlockSpec((1, gw, gw), lambda b, g: (g, 0, 0)),
                  pl.BlockSpec((1, gw), lambda b, g: (0, g))],
        out_specs=pl.BlockSpec((seq, gw), lambda b, g: (b, g)),
        out_shape=jax.ShapeDtypeStruct((n, width), BF16),
        scratch_shapes=[pltpu.VMEM((seq + _POOL_PAD, gw), F32)],
        compiler_params=_params("parallel", "arbitrary"),
        name="pool_prompt",
    )(u, w_pool, pool_scale.reshape(1, width))


def _pool_sample_body(st_ref, u_ref, w_ref, sc_ref, o_ref, *, n_state, width):
    ng = len(POOL_WINDOWS)
    gw = width // ng
    for gi, win in enumerate(POOL_WINDOWS):
        c0 = gi * gw
        cur = u_ref[:, c0:c0 + gw]
        tot = cur
        for i in range(1, win):
            r = n_state - i
            tot = tot + st_ref[:, r * width + c0:r * width + c0 + gw]
        d = tot / float(win) - cur
        y = _dot(d.astype(BF16), w_ref[gi].astype(BF16)) * sc_ref[:, c0:c0 + gw]
        o_ref[:, c0:c0 + gw] = y.astype(o_ref.dtype)


def _pool_sample(state, u, w_pool, pool_scale):
    bd, n_state, width = state.shape
    body = functools.partial(_pool_sample_body, n_state=n_state, width=width)
    return pl.pallas_call(
        body,
        out_shape=jax.ShapeDtypeStruct((bd, width), BF16),
        compiler_params=pltpu.CompilerParams(vmem_limit_bytes=VMEM_LIMIT_BYTES),
        name="pool_sample",
    )(state.reshape(bd, n_state * width), u, w_pool, pool_scale.reshape(1, width))


def _rel_bucket(n):
    max_exact = NUM_BUCKETS // 2
    nf = jnp.maximum(n, 1).astype(F32)
    large = max_exact + (jnp.log(nf / max_exact) / math.log(MAX_DISTANCE / max_exact)
                         * (NUM_BUCKETS - max_exact)).astype(jnp.int32)
    large = jnp.minimum(large, NUM_BUCKETS - 1)
    return jnp.where(n < max_exact, n, large)


def _bias_by_distance(rel_bias, n_dist):
    return rel_bias.astype(F32)[_rel_bucket(jnp.arange(n_dist))]


def _lambda(lv_ref, lam_init):
    lv = lv_ref[...]
    a = jnp.sum(lv[0:1] * lv[1:2], axis=-1, keepdims=True)
    b = jnp.sum(lv[2:3] * lv[3:4], axis=-1, keepdims=True)
    return jnp.exp(a) - jnp.exp(b) + lam_init


def _attn_prompt_body(lv_ref, q_ref, k_ref, v_ref, bias_ref, g_ref, o_ref,
                      vt_sc, m_sc, l_sc, acc_sc, *, tq, tk, seq, lam_init):
    qi = pl.program_id(2)

    @pl.when(qi == 0)
    def _transpose_values():
        for c in range(seq // tk):
            vt_sc[c] = v_ref[c * tk:(c + 1) * tk, :].T.astype(BF16)

    q = q_ref[...] * (DQK ** -0.5)
    lane = lax.broadcasted_iota(jnp.int32, q.shape, 1)
    q_ext = jnp.concatenate([jnp.where(lane < DQK, q, 0.0),
                             jnp.where(lane >= DQK, q, 0.0)], axis=0).astype(BF16)
    m_sc[...] = jnp.full(m_sc.shape, -jnp.inf, F32)
    l_sc[...] = jnp.zeros(l_sc.shape, F32)
    acc_sc[...] = jnp.zeros(acc_sc.shape, F32)

    def step(j, carry):
        r0 = pl.multiple_of(j * tk, tk)
        kj = k_ref[pl.ds(r0, tk), :].astype(BF16)
        st = lax.dot_general(kj, q_ext, (((1,), (1,)), ((), ())),
                             preferred_element_type=F32)
        bt = bias_ref[0, jnp.minimum(qi - j, 2)]
        st = st + jnp.concatenate([bt, bt], axis=1)
        m_old = m_sc[...]
        m_new = jnp.maximum(m_old, jnp.max(st, axis=0, keepdims=True))
        alpha = jnp.exp(m_old - m_new)
        p = jnp.exp(st - m_new)
        l_sc[...] = alpha * l_sc[...] + jnp.sum(p, axis=0, keepdims=True)
        acc_sc[...] = alpha * acc_sc[...] + _dot(vt_sc[j], p.astype(BF16))
        m_sc[...] = m_new
        return carry

    lax.fori_loop(0, qi + 1, step, 0)

    lam = _lambda(lv_ref, lam_init)
    o = acc_sc[...] / l_sc[...]
    att = (o[:, :tq] - lam * o[:, tq:]).T
    ms = jnp.mean(att * att, axis=-1, keepdims=True)
    out = att * lax.rsqrt(ms + EPS) * g_ref[...] * (1.0 - lam_init)
    o_ref[...] = out.astype(o_ref.dtype)


def _toeplitz(g, n):
    h = g.shape[0]
    gp = jnp.pad(g, ((0, 0), (0, 1)))
    flat = jnp.tile(gp, (1, n))[:, :n * (2 * n - 1)]
    return flat.reshape(h, n, 2 * n - 1)[:, :, n - 1:]


def _attn_prompt(q, k, v, lam_vecs, rel_bias, subln_g, batch, seq, lam_init, tq=256):
    n, width = q.shape
    heads = width // DV
    tk = tq
    nq = seq // tq
    assert tq == tk and tq >= MAX_DISTANCE
    bd = _bias_by_distance(rel_bias, 2 * tq).T
    masked = jnp.full((heads, tq - 1), -jnp.inf, F32)
    tile0 = _toeplitz(jnp.concatenate([masked, bd[:, :tq]], axis=1), tq)
    tile1 = _toeplitz(bd[:, 1:2 * tq], tq)
    tile2 = jnp.broadcast_to(bd[:, 2 * tq - 1][:, None, None], (heads, tk, tq))
    bias_tiles = jnp.stack([tile0, tile1, tile2], axis=1)
    body = functools.partial(_attn_prompt_body, tq=tq, tk=tk, seq=seq, lam_init=lam_init)
    return pl.pallas_call(
        body,
        grid=(batch, heads, nq),
        in_specs=[pl.BlockSpec((4, DQK), lambda b, h, i: (0, 0)),
                  pl.BlockSpec((tq, DV), lambda b, h, i: (b * nq + i, h)),
                  pl.BlockSpec((seq, DV), lambda b, h, i: (b, h)),
                  pl.BlockSpec((seq, DV), lambda b, h, i: (b, h)),
                  pl.BlockSpec((1, 3, tk, tq), lambda b, h, i: (h, 0, 0, 0)),
                  pl.BlockSpec((1, DV), lambda b, h, i: (0, 0))],
        out_specs=pl.BlockSpec((tq, DV), lambda b, h, i: (b * nq + i, h)),
        out_shape=jax.ShapeDtypeStruct((n, width), BF16),
        scratch_shapes=[pltpu.VMEM((seq // tk, DV, tk), BF16),
                        pltpu.VMEM((1, 2 * tq), F32), pltpu.VMEM((1, 2 * tq), F32),
                        pltpu.VMEM((DV, 2 * tq), F32)],
        compiler_params=_params("parallel", "parallel", "arbitrary"),
        name="attn_prompt",
    )(lam_vecs, q, k, v, bias_tiles, subln_g.reshape(1, DV))


_PAGES_PER_STEP = 8


def _attn_decode_body(pt_ref, lv_ref, q_ref, kn_ref, vn_ref, bias_ref, bias0_ref, g_ref,
                      *rest, n_pages, heads, lam_init):
    pp = _PAGES_PER_STEP
    k_refs = rest[:pp]
    v_refs = rest[pp:2 * pp]
    o_ref = rest[2 * pp]
    p_sc, m_sc, l_sc, f_sc, wn_sc, acc_sc = rest[2 * pp + 1:]
    del pt_ref
    nk = n_pages // pp
    s = pl.program_id(1)
    q = q_ref[0] * (DQK ** -0.5)
    lane = lax.broadcasted_iota(jnp.int32, q.shape, 1)
    q_all = jnp.concatenate([jnp.where(lane < DQK, q, 0.0),
                             jnp.where(lane >= DQK, q, 0.0)], axis=0)
    stat = m_sc.shape[1:]

    @pl.when(s < nk)
    def _scores():
        qb = q_all.astype(BF16)
        for r in range(pp):
            pg = s * pp + r
            sc = lax.dot_general(qb, k_refs[r][0].astype(BF16), (((1,), (1,)), ((), ())),
                                 preferred_element_type=F32)
            sc = sc + bias_ref[pg]
            m_pg = jnp.max(sc, axis=1, keepdims=True)
            p = jnp.exp(sc - m_pg)
            p_sc[pg] = p
            m_sc[pg] = jnp.broadcast_to(m_pg, stat)
            l_sc[pg] = jnp.broadcast_to(jnp.sum(p, axis=1, keepdims=True), stat)

    @pl.when(s == nk - 1)
    def _merge():
        kn = kn_ref[0]
        sn = (jnp.sum(q_all * jnp.concatenate([kn, kn], axis=0), axis=1, keepdims=True)
              + bias0_ref[...])
        m_all = m_sc[...]
        m = jnp.maximum(jnp.max(m_all, axis=0), sn)
        e = jnp.exp(m_all - m[None])
        pn = jnp.exp(sn - m)
        l = jnp.sum(l_sc[...] * e, axis=0) + pn
        lam = _lambda(lv_ref, lam_init)
        r = jnp.concatenate([1.0 / l[:heads], lam / l[heads:]], axis=0)
        f_sc[...] = e * r[None]
        wn = pn * r
        wn_sc[...] = wn[:heads] - wn[heads:]
        acc_sc[...] = jnp.zeros(acc_sc.shape, F32)

    @pl.when(s >= nk)
    def _values():
        tot = acc_sc[...]
        reps = p_sc.shape[2] // LANES
        for r in range(pp):
            pg = (s - nk) * pp + r
            pw = p_sc[pg] * jnp.tile(f_sc[pg], (1, reps))
            w = (pw[:heads] - pw[heads:]).astype(BF16)
            tot = tot + _dot(w, v_refs[r][0].astype(BF16))
        acc_sc[...] = tot

    @pl.when(s == 2 * nk - 1)
    def _finish():
        o = acc_sc[...] + wn_sc[...] * vn_ref[0]
        ms = jnp.mean(o * o, axis=-1, keepdims=True)
        o_ref[0] = o * lax.rsqrt(ms + EPS) * g_ref[...] * (1.0 - lam_init)


def _attn_decode(q, k_new, v_new, cache_k, cache_v, layer, page_table, lam_vecs, rel_bias,
                 subln_g, lam_init):
    bd, width = q.shape
    heads = width // DV
    n_pages = page_table.shape[1]
    past = n_pages * PAGE_SIZE
    pp = _PAGES_PER_STEP
    nk = n_pages // pp
    depth, n_pool = cache_k.shape[:2]
    ck = cache_k.reshape(depth * n_pool, PAGE_SIZE * heads, DV)
    cv = cache_v.reshape(depth * n_pool, PAGE_SIZE * heads, DV)
    page0 = layer * n_pool
    bdist = _bias_by_distance(rel_bias, past + 1)
    b_past = bdist[1:][::-1].reshape(n_pages, 1, PAGE_SIZE, heads)
    own_head = jnp.eye(heads, dtype=bool)[None, :, None, :]
    b_past = jnp.where(own_head, b_past, -jnp.inf).reshape(n_pages, heads, PAGE_SIZE * heads)
    b_past = jnp.concatenate([b_past, b_past], axis=1)
    b_new = jnp.broadcast_to(jnp.tile(bdist[0], 2)[:, None], (2 * heads, LANES))

    def k_map(r):
        return lambda b, s, pt: (page0 + pt[b, jnp.minimum(s, nk - 1) * pp + r], 0, 0)

    def v_map(r):
        return lambda b, s, pt: (page0 + pt[b, jnp.maximum(s - nk, 0) * pp + r], 0, 0)

    row3 = lambda b, s, pt: (b, 0, 0)
    const2 = lambda b, s, pt: (0, 0)
    page_rows = PAGE_SIZE * heads
    in_specs = [pl.BlockSpec((4, DQK), const2),
                pl.BlockSpec((1, heads, DV), row3),
                pl.BlockSpec((1, heads, DV), row3),
                pl.BlockSpec((1, heads, DV), row3),
                pl.BlockSpec((n_pages, 2 * heads, page_rows), lambda b, s, pt: (0, 0, 0),
                             pipeline_mode=pl.Buffered(1)),
                pl.BlockSpec((2 * heads, LANES), const2),
                pl.BlockSpec((1, DV), const2)]
    in_specs += [pl.BlockSpec((1, page_rows, DV), k_map(r)) for r in range(pp)]
    in_specs += [pl.BlockSpec((1, page_rows, DV), v_map(r)) for r in range(pp)]
    body = functools.partial(_attn_decode_body, n_pages=n_pages, heads=heads, lam_init=lam_init)
    out = pl.pallas_call(
        body,
        grid_spec=pltpu.PrefetchScalarGridSpec(
            num_scalar_prefetch=1,
            grid=(bd, 2 * nk),
            in_specs=in_specs,
            out_specs=pl.BlockSpec((1, heads, DV), row3),
            scratch_shapes=[pltpu.VMEM((n_pages, 2 * heads, page_rows), F32),
                            pltpu.VMEM((n_pages, 2 * heads, LANES), F32),
                            pltpu.VMEM((n_pages, 2 * heads, LANES), F32),
                            pltpu.VMEM((n_pages, 2 * heads, LANES), F32),
                            pltpu.VMEM((heads, LANES), F32),
                            pltpu.VMEM((heads, DV), F32)]),
        out_shape=jax.ShapeDtypeStruct((bd, heads, DV), F32),
        compiler_params=_params("parallel", "arbitrary"),
        name="attn_decode",
    )(page_table, lam_vecs, q.reshape(bd, heads, DV), k_new.reshape(bd, heads, DV),
      v_new.reshape(bd, heads, DV), b_past, b_new, subln_g.reshape(1, DV),
      *([ck] * pp), *([cv] * pp))
    return out.reshape(bd, width)


def _topk_rows(s, k):
    rows = s.shape[0]
    iota = lax.broadcasted_iota(jnp.int32, s.shape, 0).astype(F32)
    rank = jnp.full(s.shape, float(k), F32)
    vals = []
    for p in range(k):
        m = jnp.max(s, axis=0, keepdims=True)
        idx = jnp.min(jnp.where(s == m, iota, float(rows)), axis=0, keepdims=True)
        sel = iota == idx
        rank = jnp.where(sel, float(p), rank)
        s = jnp.where(sel, -jnp.inf, s)
        vals.append(m)
    return vals, rank


def _route_body(qr_ref, keys_ref, n1_ref, w1_ref, r2_ref, w2_ref):
    k = PEER_TOPK
    nt = (((1,), (1,)), ((), ()))
    s1 = lax.dot_general(keys_ref[0], qr_ref[:, 0:LANES], nt, preferred_element_type=F32)
    s2 = lax.dot_general(keys_ref[1], qr_ref[:, LANES:2 * LANES], nt, preferred_element_type=F32)
    v1, rank1 = _topk_rows(s1, k)
    v2, rank2 = _topk_rows(s2, k)
    v1_all = jnp.concatenate(v1, axis=0)
    v2_all = jnp.concatenate(v2, axis=0)
    sub = lax.broadcasted_iota(jnp.int32, (8, s1.shape[1]), 0)
    blocks, spans, row = [], [], 0
    p = 0
    while p < k:
        nq = k // (p + 1)
        if nq >= 8:
            nb = -(-nq // 8) * 8
            blk = v1[p] + v2_all[0:nb]
            if nb != nq:
                blk = jnp.where(jnp.concatenate([sub + 8 * i for i in range(nb // 8)]) < nq,
                                blk, -jnp.inf)
            blocks.append(blk)
            spans.append((p, row, nb))
            row += nb
            p += 1
        elif nq > 1:
            blocks.append(jnp.where(sub < nq, v1[p] + v2_all[0:8], -jnp.inf))
            spans.append((p, row, 8))
            row += 8
            p += 1
        else:
            assert (k - p) == 8, "single-candidate rows are packed as one aligned 8-row block"
            blocks.append(v1_all[p:p + 8] + v2[0])
            for r in range(8):
                spans.append((p + r, row + r, 1))
            row += 8
            p += 8
    cand = jnp.concatenate(blocks, axis=0)
    top, rank_c = _topk_rows(cand, k)
    chosen = jnp.where(rank_c < float(k), 1.0, 0.0)
    z = jnp.zeros_like(top[0])
    for i in range(k):
        z = z + jnp.exp(top[i] - top[0])
    n1 = jnp.zeros(s1.shape, F32)
    for pp, r0, nr in spans:
        n_p = jnp.sum(chosen[r0:r0 + nr], axis=0, keepdims=True)
        n1 = jnp.where(rank1 == float(pp), n_p, n1)
    n1_ref[0] = n1
    w1_ref[0] = jnp.exp(s1 - v1[0]) / z
    r2_ref[0] = rank2
    w2_ref[0] = jnp.exp(s2 - v2[0])


def _route(qr, sub_keys, tt):
    n = qr.shape[0]
    heads, _, n_keys, d_half = sub_keys.shape
    keys = sub_keys.reshape(heads * 2, n_keys, d_half).astype(BF16)
    out = jax.ShapeDtypeStruct((heads, n_keys, n), F32)
    ospec = pl.BlockSpec((1, n_keys, tt), lambda i, h: (h, 0, i))
    return pl.pallas_call(
        _route_body,
        grid=(n // tt, heads),
        in_specs=[pl.BlockSpec((tt, 2 * d_half), lambda i, h: (i, h)),
                  pl.BlockSpec((2, n_keys, d_half), lambda i, h: (h, 0, 0))],
        out_specs=[ospec, ospec, ospec, ospec],
        out_shape=[out, out, out, out],
        compiler_params=_params("parallel", "arbitrary"),
        name="peer_route",
    )(qr, keys)


def _gelu(x):
    return 0.5 * x * (1.0 + lax.erf(x * (2.0 ** -0.5)))


def _peer_body(h_ref, u_ref, v_ref, n1_ref, w1_ref, r2_ref, w2_ref, x_ref, o_ref,
               z0_sc, z1_sc, a_sc, *, ec, tt, heads, n_keys):
    j = pl.program_id(1)
    nsl = ec // n_keys

    @pl.when(j == 0)
    def _init():
        o_ref[...] = x_ref[...]
        z1_sc[...] = jnp.zeros(z1_sc.shape, F32)

    def stage(z_next, z_prev):
        z_next[...] = lax.dot_general(u_ref[...], h_ref[...], (((1,), (1,)), ((), ())),
                                      preferred_element_type=F32)
        for sl in range(nsl):
            for tb in range(tt // LANES):
                cols = slice(tb * LANES, (tb + 1) * LANES)
                g = jnp.zeros((n_keys, LANES), F32)
                for h in range(heads):
                    row = h * nsl + sl
                    n1 = n1_ref[0, row:row + 1, cols]
                    w1 = w1_ref[0, row:row + 1, cols]
                    g = g + jnp.where(r2_ref[h, :, cols] < n1, w2_ref[h, :, cols] * w1, 0.0)
                z = z_prev[sl * n_keys:(sl + 1) * n_keys, cols]
                a_sc[sl * n_keys:(sl + 1) * n_keys, cols] = (_gelu(z) * g).astype(a_sc.dtype)
        o_ref[...] += lax.dot_general(a_sc[...], v_ref[...], (((0,), (0,)), ((), ())),
                                      preferred_element_type=F32)

    @pl.when(j % 2 == 0)
    def _even():
        stage(z0_sc, z1_sc)

    @pl.when(j % 2 == 1)
    def _odd():
        stage(z1_sc, z0_sc)


def _peer(h2, u_bf, v_bf, tables, x, tt, ec):
    n, d = h2.shape
    n_exp = u_bf.shape[0]
    n1, w1, r2, w2 = tables
    heads, n_keys, _ = n1.shape
    nsl = ec // n_keys
    nc = n_exp // ec

    def by_chunk(t):
        t = t.reshape(heads, n_keys // nsl, nsl, n)
        return jnp.transpose(t, (1, 0, 2, 3)).reshape(n_keys // nsl, heads * nsl, n)

    once = pl.Buffered(1)
    tspec = pl.BlockSpec((heads, n_keys, tt), lambda i, j: (0, 0, i), pipeline_mode=once)
    prev = lambda j: jnp.maximum(j - 1, 0)
    cspec = pl.BlockSpec((1, heads * nsl, tt), lambda i, j: (prev(j), 0, i))
    body = functools.partial(_peer_body, ec=ec, tt=tt, heads=heads, n_keys=n_keys)
    return pl.pallas_call(
        body,
        grid=(n // tt, nc + 1),
        in_specs=[pl.BlockSpec((tt, d), lambda i, j: (i, 0), pipeline_mode=once),
                  pl.BlockSpec((ec, d), lambda i, j: (jnp.minimum(j, nc - 1), 0)),
                  pl.BlockSpec((ec, d), lambda i, j: (prev(j), 0)),
                  cspec, cspec, tspec, tspec,
                  pl.BlockSpec((tt, d), lambda i, j: (i, 0), pipeline_mode=once)],
        out_specs=pl.BlockSpec((tt, d), lambda i, j: (i, 0)),
        out_shape=jax.ShapeDtypeStruct((n, d), F32),
        scratch_shapes=[pltpu.VMEM((ec, tt), F32), pltpu.VMEM((ec, tt), F32),
                        pltpu.VMEM((ec, tt), BF16)],
        compiler_params=_params("parallel", "arbitrary"),
        name="peer_experts",
    )(h2, u_bf, v_bf, by_chunk(n1), by_chunk(w1), r2, w2, x)


def _tile(n, pref):
    return pref if n % pref == 0 else n


def _project(x2d, ln_g, w_in_bf, qn_g, kn_g, widths):
    pool_w, attn_w = widths
    n, d = x2d.shape
    tm = _tile(n, 1024)
    xn = _rmsnorm(x2d, ln_g, _tile(n, 256))
    mm = functools.partial(_matmul, [xn], [(w_in_bf, 0, d)], tm=tm, tn=512, out_dtype=F32)
    u = mm(n_out=pool_w, w_col_off=0, name="proj_u")
    q = mm(n_out=attn_w, w_col_off=pool_w, gain=qn_g, qk_group=DQK, name="proj_q")
    k = mm(n_out=attn_w, w_col_off=pool_w + attn_w, gain=kn_g, qk_group=DQK, name="proj_k")
    v = mm(n_out=attn_w, w_col_off=pool_w + 2 * attn_w, name="proj_v")
    return u, q, k, v


def _mix_and_peer(x2d, pool_y, att, w_o_bf, ln2_g, w_query_bf, sub_keys, u_bf, v_bf):
    n, d = x2d.shape
    pool_w = pool_y.shape[1]
    tm = _tile(n, 1024)
    assert att.shape[1] == pool_w, "w_o row blocks assume equal pool and attention widths"
    x1 = _matmul([pool_y, att], [(w_o_bf, 0, pool_w), (w_o_bf, 1, pool_w)],
                 n_out=d, tm=tm, tn=512, out_dtype=F32, res=x2d, name="out_proj")
    h2 = _rmsnorm(x1, ln2_g, _tile(n, 256))
    qr = _matmul([h2], [(w_query_bf, 0, d)], n_out=w_query_bf.shape[1], tm=tm, tn=512,
                 out_dtype=BF16, name="peer_query")
    tables = _route(qr, sub_keys, _tile(n, LANES))
    return _peer(h2, u_bf, v_bf, tables, x1, _tile(n, 512), 256)


def kernel(x_prompt, x_sample, cache_k, cache_v, state_pool, page_table, rel_bias, ln1_g, w_in, q_norm_g, k_norm_g, lambda_q1, lambda_k1, lambda_q2, lambda_k2, subln_g, w_pool, pool_scale, w_o, ln2_g, w_query, sub_keys, peer_u, peer_v):
    batch, seq, d = x_prompt.shape
    bd, tq, _ = x_sample.shape
    assert tq == 1, "sample group is one new position per sequence"
    depth = w_in.shape[0]
    pool_w = w_pool.shape[1] * w_pool.shape[2]
    attn_w = (w_in.shape[2] - pool_w) // 3
    heads = attn_w // DV
    n_state = state_pool.shape[2]

    xp = x_prompt.reshape(batch * seq, d)
    xs = x_sample.reshape(bd, d)
    kp_l, vp_l, pp_l, ks_l, vs_l, ps_l = [], [], [], [], [], []
    for l in range(depth):
        lam_init = 0.8 - 0.6 * math.exp(-0.3 * l)
        lam_vecs = jnp.stack([lambda_q1[l], lambda_k1[l], lambda_q2[l], lambda_k2[l]]).astype(F32)
        w_in_bf = w_in[l].astype(BF16)
        w_o_bf = w_o[l].astype(BF16)
        w_query_bf = w_query[l].astype(BF16)
        u_bf = peer_u[l].astype(BF16)
        v_bf = peer_v[l].astype(BF16)

        u, q, k, v = _project(xp, ln1_g[l], w_in_bf, q_norm_g[l], k_norm_g[l], (pool_w, attn_w))
        pool_y = _pool_prompt(u, w_pool[l], pool_scale[l], batch, seq)
        att = _attn_prompt(q, k, v, lam_vecs, rel_bias, subln_g[l], batch, seq, lam_init)
        xp = _mix_and_peer(xp, pool_y, att, w_o_bf, ln2_g[l], w_query_bf, sub_keys[l], u_bf, v_bf)
        kp_l.append(k.reshape(batch, seq, heads, 2 * DQK))
        vp_l.append(v.reshape(batch, seq, heads, DV))
        pp_l.append(u.reshape(batch, seq, pool_w)[:, seq - n_state:])

        u, q, k, v = _project(xs, ln1_g[l], w_in_bf, q_norm_g[l], k_norm_g[l], (pool_w, attn_w))
        pool_y = _pool_sample(state_pool[l], u, w_pool[l], pool_scale[l])
        att = _attn_decode(q, k, v, cache_k, cache_v, l, page_table, lam_vecs, rel_bias,
                           subln_g[l], lam_init).astype(BF16)
        xs = _mix_and_peer(xs, pool_y, att, w_o_bf, ln2_g[l], w_query_bf, sub_keys[l], u_bf, v_bf)
        ks_l.append(k.reshape(bd, tq, heads, 2 * DQK))
        vs_l.append(v.reshape(bd, tq, heads, DV))
        ps_l.append(jnp.concatenate([state_pool[l], u[:, None, :]], axis=1)[:, -n_state:])

    return (xp.reshape(batch, seq, d), xs.reshape(bd, tq, d),
            jnp.stack(kp_l), jnp.stack(vp_l), jnp.stack(pp_l),
            jnp.stack(ks_l), jnp.stack(vs_l), jnp.stack(ps_l))
```

```python
import functools
import math

import jax
import jax.numpy as jnp
from jax import lax
from jax.experimental import pallas as pl
from jax.experimental.pallas import tpu as pltpu

F32 = jnp.float32
BF16 = jnp.bfloat16
EPS = 1e-6

LANES = 128
VMEM_LIMIT_BYTES = 56 * 1024 * 1024

POOL_WINDOWS = (2, 4, 8, 16)
DQK = 64
DV = 128
NUM_BUCKETS = 32
MAX_DISTANCE = 128
PEER_TOPK = 16
PAGE_SIZE = 128


def _params(*sem, flags=None):
    return pltpu.CompilerParams(dimension_semantics=sem, vmem_limit_bytes=VMEM_LIMIT_BYTES,
                                flags=flags)


def _split_bf16(x):
    hi = x.astype(BF16)
    lo = (x - hi.astype(F32)).astype(BF16)
    return hi, lo


def _dot(a, b):
    return jnp.dot(a, b, preferred_element_type=F32)


def _dot2(x, w):
    hi, lo = _split_bf16(x)
    return _dot(hi, w) + _dot(lo, w)


def _rmsnorm_body(x_ref, g_ref, o_ref):
    x = x_ref[...]
    ms = jnp.mean(x * x, axis=-1, keepdims=True)
    o_ref[...] = (x * lax.rsqrt(ms + EPS) * g_ref[...]).astype(o_ref.dtype)


def _rmsnorm(x, g, tm):
    m, d = x.shape
    return pl.pallas_call(
        _rmsnorm_body,
        grid=(m // tm,),
        in_specs=[pl.BlockSpec((tm, d), lambda i: (i, 0)),
                  pl.BlockSpec((1, d), lambda i: (0, 0))],
        out_specs=pl.BlockSpec((tm, d), lambda i: (i, 0)),
        out_shape=jax.ShapeDtypeStruct((m, d), BF16),
        compiler_params=_params("parallel"),
        name="rmsnorm",
    )(x, g.reshape(1, d))


def _mm_body(*refs, n_pairs, qk_group, has_res):
    a_refs = refs[:n_pairs]
    w_refs = refs[n_pairs:2 * n_pairs]
    pos = 2 * n_pairs
    if qk_group:
        g_ref, ones_ref = refs[pos], refs[pos + 1]
        pos += 2
    if has_res:
        r_ref = refs[pos]
        pos += 1
    o_ref = refs[pos]

    acc = _dot(a_refs[0][...], w_refs[0][...])
    for p in range(1, n_pairs):
        acc = acc + _dot(a_refs[p][...], w_refs[p][...])
    if qk_group:
        sq = acc * acc
        hi, lo = _split_bf16(sq)
        ones = ones_ref[...]
        parts = []
        for s in range(acc.shape[1] // LANES):
            sl = slice(s * LANES, (s + 1) * LANES)
            parts.append(_dot(hi[:, sl], ones) + _dot(lo[:, sl], ones))
        gs = jnp.concatenate(parts, axis=1)
        acc = acc * lax.rsqrt(gs * (1.0 / qk_group) + EPS) * g_ref[...]
    if has_res:
        acc = acc + r_ref[...]
    o_ref[...] = acc.astype(o_ref.dtype)


def _matmul(a_list, w_list, *, n_out, tm, tn, out_dtype, w_col_off=0,
            gain=None, qk_group=0, res=None, name="matmul"):
    m = a_list[0].shape[0]
    tn = _tile(n_out, tn)
    off = w_col_off // tn
    in_specs, args = [], []
    for a in a_list:
        in_specs.append(pl.BlockSpec((tm, a.shape[1]), lambda i, j: (i, 0)))
        args.append(a)
    for w, rb, kr in w_list:
        in_specs.append(pl.BlockSpec((kr, tn), lambda i, j, rb=rb: (rb, j + off)))
        args.append(w)
    if qk_group:
        lane = jnp.arange(LANES)
        ones_bd = (lane[:, None] // qk_group == lane[None, :] // qk_group).astype(BF16)
        in_specs += [pl.BlockSpec((1, tn), lambda i, j: (0, 0)),
                     pl.BlockSpec((LANES, LANES), lambda i, j: (0, 0))]
        args += [jnp.tile(gain.astype(F32), tn // gain.shape[0]).reshape(1, tn), ones_bd]
    if res is not None:
        in_specs.append(pl.BlockSpec((tm, tn), lambda i, j: (i, j)))
        args.append(res)
    body = functools.partial(_mm_body, n_pairs=len(a_list), qk_group=qk_group,
                             has_res=res is not None)
    return pl.pallas_call(
        body,
        grid=(m // tm, n_out // tn),
        in_specs=in_specs,
        out_specs=pl.BlockSpec((tm, tn), lambda i, j: (i, j)),
        out_shape=jax.ShapeDtypeStruct((m, n_out), out_dtype),
        compiler_params=_params("parallel", "arbitrary"),
        name=name,
    )(*args)


_POOL_PAD = 16
_POOL_ROWS = 256


def _pool_prompt_body(u_ref, w_ref, sc_ref, o_ref, pad_ref, *, seq):
    g = pl.program_id(1)
    pad_ref[0:_POOL_PAD, :] = jnp.zeros((_POOL_PAD, pad_ref.shape[1]), F32)
    pad_ref[_POOL_PAD:, :] = u_ref[...]
    wmat = w_ref[0].astype(BF16)
    scale = sc_ref[...]
    for gi, win in enumerate(POOL_WINDOWS):
        @pl.when(g == gi)
        def _(win=win):
            for r0 in range(0, seq, _POOL_ROWS):
                cur = pad_ref[r0 + _POOL_PAD:r0 + _POOL_PAD + _POOL_ROWS, :]
                tot = cur
                for i in range(1, win):
                    tot = tot + pad_ref[r0 + _POOL_PAD - i:r0 + _POOL_PAD - i + _POOL_ROWS, :]
                pos = r0 + lax.broadcasted_iota(jnp.int32, (_POOL_ROWS, 1), 0)
                cnt = jnp.minimum(win, pos + 1).astype(F32)
                d = tot / cnt - cur
                y = _dot(d.astype(BF16), wmat) * scale
                o_ref[r0:r0 + _POOL_ROWS, :] = y.astype(o_ref.dtype)


def _pool_prompt(u, w_pool, pool_scale, batch, seq):
    n, width = u.shape
    ng = len(POOL_WINDOWS)
    gw = width // ng
    body = functools.partial(_pool_prompt_body, seq=seq)
    return pl.pallas_call(
        body,
        grid=(batch, ng),
        in_specs=[pl.BlockSpec((seq, gw), lambda b, g: (b, g)),
                  pl.BlockSpec((1, gw, gw), lambda b, g: (g, 0, 0)),
                  pl.BlockSpec((1, gw), lambda b, g: (0, g))],
        out_specs=pl.BlockSpec((seq, gw), lambda b, g: (b, g)),
        out_shape=jax.ShapeDtypeStruct((n, width), BF16),
        scratch_shapes=[pltpu.VMEM((seq + _POOL_PAD, gw), F32)],
        compiler_params=_params("parallel", "arbitrary"),
        name="pool_prompt",
    )(u, w_pool, pool_scale.reshape(1, width))


def _pool_sample_body(st_ref, u_ref, w_ref, sc_ref, o_ref, *, n_state, width):
    ng = len(POOL_WINDOWS)
    gw = width // ng
    for gi, win in enumerate(POOL_WINDOWS):
        c0 = gi * gw
        cur = u_ref[:, c0:c0 + gw]
        tot = cur
        for i in range(1, win):
            r = n_state - i
            tot = tot + st_ref[:, r * width + c0:r * width + c0 + gw]
        d = tot / float(win) - cur
        y = _dot(d.astype(BF16), w_ref[gi].astype(BF16)) * sc_ref[:, c0:c0 + gw]
        o_ref[:, c0:c0 + gw] = y.astype(o_ref.dtype)


def _pool_sample(state, u, w_pool, pool_scale):
    bd, n_state, width = state.shape
    body = functools.partial(_pool_sample_body, n_state=n_state, width=width)
    return pl.pallas_call(
        body,
        out_shape=jax.ShapeDtypeStruct((bd, width), BF16),
        compiler_params=pltpu.CompilerParams(vmem_limit_bytes=VMEM_LIMIT_BYTES),
        name="pool_sample",
    )(state.reshape(bd, n_state * width), u, w_pool, pool_scale.reshape(1, width))


def _rel_bucket(n):
    max_exact = NUM_BUCKETS // 2
    nf = jnp.maximum(n, 1).astype(F32)
    large = max_exact + (jnp.log(nf / max_exact) / math.log(MAX_DISTANCE / max_exact)
                         * (NUM_BUCKETS - max_exact)).astype(jnp.int32)
    large = jnp.minimum(large, NUM_BUCKETS - 1)
    return jnp.where(n < max_exact, n, large)


def _bias_by_distance(rel_bias, n_dist):
    return rel_bias.astype(F32)[_rel_bucket(jnp.arange(n_dist))]


def _lambda(lv_ref, lam_init):
    lv = lv_ref[...]
    a = jnp.sum(lv[0:1] * lv[1:2], axis=-1, keepdims=True)
    b = jnp.sum(lv[2:3] * lv[3:4], axis=-1, keepdims=True)
    return jnp.exp(a) - jnp.exp(b) + lam_init


def _attn_prompt_body(lv_ref, q_ref, k_ref, v_ref, bias_ref, g_ref, o_ref,
                      vt_sc, m_sc, l_sc, acc_sc, *, tq, tk, seq, lam_init):
    qi = pl.program_id(2)

    @pl.when(qi == 0)
    def _transpose_values():
        for c in range(seq // tk):
            vt_sc[c] = v_ref[c * tk:(c + 1) * tk, :].T.astype(BF16)

    q = q_ref[...] * (DQK ** -0.5)
    lane = lax.broadcasted_iota(jnp.int32, q.shape, 1)
    q_ext = jnp.concatenate([jnp.where(lane < DQK, q, 0.0),
                             jnp.where(lane >= DQK, q, 0.0)], axis=0).astype(BF16)
    m_sc[...] = jnp.full(m_sc.shape, -jnp.inf, F32)
    l_sc[...] = jnp.zeros(l_sc.shape, F32)
    acc_sc[...] = jnp.zeros(acc_sc.shape, F32)

    def step(j, carry):
        r0 = pl.multiple_of(j * tk, tk)
        kj = k_ref[pl.ds(r0, tk), :].astype(BF16)
        st = lax.dot_general(kj, q_ext, (((1,), (1,)), ((), ())),
                             preferred_element_type=F32)
        bt = bias_ref[0, jnp.minimum(qi - (tk // tq) * j, 3)]
        st = st + jnp.concatenate([bt, bt], axis=1)
        m_old = m_sc[...]
        m_new = jnp.maximum(m_old, jnp.max(st, axis=0, keepdims=True))
        alpha = jnp.exp(m_old - m_new)
        p = jnp.exp(st - m_new)
        l_sc[...] = alpha * l_sc[...] + jnp.sum(p, axis=0, keepdims=True)
        acc_sc[...] = alpha * acc_sc[...] + _dot(vt_sc[j], p.astype(BF16))
        m_sc[...] = m_new
        return carry

    lax.fori_loop(0, (qi * tq) // tk + 1, step, 0)

    lam = _lambda(lv_ref, lam_init)
    o = acc_sc[...] / l_sc[...]
    att = (o[:, :tq] - lam * o[:, tq:]).T
    ms = jnp.mean(att * att, axis=-1, keepdims=True)
    out = att * lax.rsqrt(ms + EPS) * g_ref[...] * (1.0 - lam_init)
    o_ref[...] = out.astype(o_ref.dtype)


def _toeplitz(g, rows, cols):
    h = g.shape[0]
    period = rows + cols
    gp = jnp.pad(g, ((0, 0), (0, 1)))
    flat = jnp.tile(gp, (1, rows))[:, :rows * (period - 1)]
    return flat.reshape(h, rows, period - 1)[:, :, rows - 1:rows - 1 + cols]


def _attn_prompt(q, k, v, lam_vecs, rel_bias, subln_g, batch, seq, lam_init, tq=256):
    n, width = q.shape
    heads = width // DV
    tk = 2 * tq
    nq = seq // tq
    assert seq % tk == 0 and tq >= MAX_DISTANCE
    n_dist = 3 * tq
    bd = _bias_by_distance(rel_bias, n_dist).T
    line = jnp.concatenate([jnp.full((heads, tk - 1), -jnp.inf, F32), bd], axis=1)
    tiles = [_toeplitz(line[:, t * tq:t * tq + tk + tq - 1], tk, tq) for t in range(3)]
    tiles.append(jnp.broadcast_to(bd[:, n_dist - 1][:, None, None], (heads, tk, tq)))
    bias_tiles = jnp.stack(tiles, axis=1)
    body = functools.partial(_attn_prompt_body, tq=tq, tk=tk, seq=seq, lam_init=lam_init)
    return pl.pallas_call(
        body,
        grid=(batch, heads, nq),
        in_specs=[pl.BlockSpec((4, DQK), lambda b, h, i: (0, 0)),
                  pl.BlockSpec((tq, DV), lambda b, h, i: (b * nq + i, h)),
                  pl.BlockSpec((seq, DV), lambda b, h, i: (b, h)),
                  pl.BlockSpec((seq, DV), lambda b, h, i: (b, h)),
                  pl.BlockSpec((1, 4, tk, tq), lambda b, h, i: (h, 0, 0, 0)),
                  pl.BlockSpec((1, DV), lambda b, h, i: (0, 0))],
        out_specs=pl.BlockSpec((tq, DV), lambda b, h, i: (b * nq + i, h)),
        out_shape=jax.ShapeDtypeStruct((n, width), BF16),
        scratch_shapes=[pltpu.VMEM((seq // tk, DV, tk), BF16),
                        pltpu.VMEM((1, 2 * tq), F32), pltpu.VMEM((1, 2 * tq), F32),
                        pltpu.VMEM((DV, 2 * tq), F32)],
        compiler_params=_params("parallel", "parallel", "arbitrary"),
        name="attn_prompt",
    )(lam_vecs, q, k, v, bias_tiles, subln_g.reshape(1, DV))


_PAGES_PER_STEP = 8


def _attn_decode_body(pt_ref, lv_ref, q_ref, kn_ref, vn_ref, bias_ref, bias0_ref, g_ref,
                      *rest, n_pages, heads, lam_init):
    pp = _PAGES_PER_STEP
    k_refs = rest[:pp]
    v_refs = rest[pp:2 * pp]
    o_ref = rest[2 * pp]
    p_sc, m_sc, l_sc, f_sc, wn_sc, acc_sc = rest[2 * pp + 1:]
    del pt_ref
    nk = n_pages // pp
    s = pl.program_id(1)
    q = q_ref[0] * (DQK ** -0.5)
    lane = lax.broadcasted_iota(jnp.int32, q.shape, 1)
    q_all = jnp.concatenate([jnp.where(lane < DQK, q, 0.0),
                             jnp.where(lane >= DQK, q, 0.0)], axis=0)
    stat = m_sc.shape[1:]

    @pl.when(s < nk)
    def _scores():
        qb = q_all.astype(BF16)
        for r in range(pp):
            pg = s * pp + r
            sc = lax.dot_general(qb, k_refs[r][0].astype(BF16), (((1,), (1,)), ((), ())),
                                 preferred_element_type=F32)
            sc = sc + bias_ref[pg]
            m_pg = jnp.max(sc, axis=1, keepdims=True)
            p = jnp.exp(sc - m_pg)
            p_sc[pg] = p
            m_sc[pg] = jnp.broadcast_to(m_pg, stat)
            l_sc[pg] = jnp.broadcast_to(jnp.sum(p, axis=1, keepdims=True), stat)

    @pl.when(s == nk - 1)
    def _merge():
        kn = kn_ref[0]
        sn = (jnp.sum(q_all * jnp.concatenate([kn, kn], axis=0), axis=1, keepdims=True)
              + bias0_ref[...])
        m_all = m_sc[...]
        m = jnp.maximum(jnp.max(m_all, axis=0), sn)
        e = jnp.exp(m_all - m[None])
        pn = jnp.exp(sn - m)
        l = jnp.sum(l_sc[...] * e, axis=0) + pn
        lam = _lambda(lv_ref, lam_init)
        r = jnp.concatenate([1.0 / l[:heads], lam / l[heads:]], axis=0)
        f_sc[...] = e * r[None]
        wn = pn * r
        wn_sc[...] = wn[:heads] - wn[heads:]
        acc_sc[...] = jnp.zeros(acc_sc.shape, F32)

    @pl.when(s >= nk)
    def _values():
        tot = acc_sc[...]
        reps = p_sc.shape[2] // LANES
        for r in range(pp):
            pg = (s - nk) * pp + r
            pw = p_sc[pg] * jnp.tile(f_sc[pg], (1, reps))
            w = (pw[:heads] - pw[heads:]).astype(BF16)
            tot = tot + _dot(w, v_refs[r][0].astype(BF16))
        acc_sc[...] = tot

    @pl.when(s == 2 * nk - 1)
    def _finish():
        o = acc_sc[...] + wn_sc[...] * vn_ref[0]
        ms = jnp.mean(o * o, axis=-1, keepdims=True)
        o_ref[0] = o * lax.rsqrt(ms + EPS) * g_ref[...] * (1.0 - lam_init)


def _attn_decode(q, k_new, v_new, cache_k, cache_v, layer, page_table, lam_vecs, rel_bias,
                 subln_g, lam_init):
    bd, width = q.shape
    heads = width // DV
    n_pages = page_table.shape[1]
    past = n_pages * PAGE_SIZE
    pp = _PAGES_PER_STEP
    nk = n_pages // pp
    depth, n_pool = cache_k.shape[:2]
    ck = cache_k.reshape(depth * n_pool, PAGE_SIZE * heads, DV)
    cv = cache_v.reshape(depth * n_pool, PAGE_SIZE * heads, DV)
    page0 = layer * n_pool
    bdist = _bias_by_distance(rel_bias, past + 1)
    b_past = bdist[1:][::-1].reshape(n_pages, 1, PAGE_SIZE, heads)
    own_head = jnp.eye(heads, dtype=bool)[None, :, None, :]
    b_past = jnp.where(own_head, b_past, -jnp.inf).reshape(n_pages, heads, PAGE_SIZE * heads)
    b_past = jnp.concatenate([b_past, b_past], axis=1)
    b_new = jnp.broadcast_to(jnp.tile(bdist[0], 2)[:, None], (2 * heads, LANES))

    def k_map(r):
        return lambda b, s, pt: (page0 + pt[b, jnp.minimum(s, nk - 1) * pp + r], 0, 0)

    def v_map(r):
        return lambda b, s, pt: (page0 + pt[b, jnp.maximum(s - nk, 0) * pp + r], 0, 0)

    row3 = lambda b, s, pt: (b, 0, 0)
    const2 = lambda b, s, pt: (0, 0)
    page_rows = PAGE_SIZE * heads
    in_specs = [pl.BlockSpec((4, DQK), const2),
                pl.BlockSpec((1, heads, DV), row3),
                pl.BlockSpec((1, heads, DV), row3),
                pl.BlockSpec((1, heads, DV), row3),
                pl.BlockSpec((n_pages, 2 * heads, page_rows), lambda b, s, pt: (0, 0, 0),
                             pipeline_mode=pl.Buffered(1)),
                pl.BlockSpec((2 * heads, LANES), const2),
                pl.BlockSpec((1, DV), const2)]
    in_specs += [pl.BlockSpec((1, page_rows, DV), k_map(r)) for r in range(pp)]
    in_specs += [pl.BlockSpec((1, page_rows, DV), v_map(r)) for r in range(pp)]
    body = functools.partial(_attn_decode_body, n_pages=n_pages, heads=heads, lam_init=lam_init)
    out = pl.pallas_call(
        body,
        grid_spec=pltpu.PrefetchScalarGridSpec(
            num_scalar_prefetch=1,
            grid=(bd, 2 * nk),
            in_specs=in_specs,
            out_specs=pl.BlockSpec((1, heads, DV), row3),
            scratch_shapes=[pltpu.VMEM((n_pages, 2 * heads, page_rows), F32),
                            pltpu.VMEM((n_pages, 2 * heads, LANES), F32),
                            pltpu.VMEM((n_pages, 2 * heads, LANES), F32),
                            pltpu.VMEM((n_pages, 2 * heads, LANES), F32),
                            pltpu.VMEM((heads, LANES), F32),
                            pltpu.VMEM((heads, DV), F32)]),
        out_shape=jax.ShapeDtypeStruct((bd, heads, DV), F32),
        compiler_params=_params("parallel", "arbitrary"),
        name="attn_decode",
    )(page_table, lam_vecs, q.reshape(bd, heads, DV), k_new.reshape(bd, heads, DV),
      v_new.reshape(bd, heads, DV), b_past, b_new, subln_g.reshape(1, DV),
      *([ck] * pp), *([cv] * pp))
    return out.reshape(bd, width)


def _topk_rows(s, k):
    rows = s.shape[0]
    iota = lax.broadcasted_iota(jnp.int32, s.shape, 0).astype(F32)
    rank = jnp.full(s.shape, float(k), F32)
    vals = []
    for p in range(k):
        m = jnp.max(s, axis=0, keepdims=True)
        idx = jnp.min(jnp.where(s == m, iota, float(rows)), axis=0, keepdims=True)
        sel = iota == idx
        rank = jnp.where(sel, float(p), rank)
        s = jnp.where(sel, -jnp.inf, s)
        vals.append(m)
    return vals, rank


def _route_body(qr_ref, keys_ref, n1_ref, w1_ref, r2_ref, w2_ref):
    k = PEER_TOPK
    nt = (((1,), (1,)), ((), ()))
    s1 = lax.dot_general(keys_ref[0], qr_ref[:, 0:LANES], nt, preferred_element_type=F32)
    s2 = lax.dot_general(keys_ref[1], qr_ref[:, LANES:2 * LANES], nt, preferred_element_type=F32)
    v1, rank1 = _topk_rows(s1, k)
    v2, rank2 = _topk_rows(s2, k)
    v1_all = jnp.concatenate(v1, axis=0)
    v2_all = jnp.concatenate(v2, axis=0)
    sub = lax.broadcasted_iota(jnp.int32, (8, s1.shape[1]), 0)
    blocks, spans, row = [], [], 0
    p = 0
    while p < k:
        nq = k // (p + 1)
        if nq >= 8:
            nb = -(-nq // 8) * 8
            blk = v1[p] + v2_all[0:nb]
            if nb != nq:
                blk = jnp.where(jnp.concatenate([sub + 8 * i for i in range(nb // 8)]) < nq,
                                blk, -jnp.inf)
            blocks.append(blk)
            spans.append((p, row, nb))
            row += nb
            p += 1
        elif nq > 1:
            blocks.append(jnp.where(sub < nq, v1[p] + v2_all[0:8], -jnp.inf))
            spans.append((p, row, 8))
            row += 8
            p += 1
        else:
            assert (k - p) == 8, "single-candidate rows are packed as one aligned 8-row block"
            blocks.append(v1_all[p:p + 8] + v2[0])
            for r in range(8):
                spans.append((p + r, row + r, 1))
            row += 8
            p += 8
    cand = jnp.concatenate(blocks, axis=0)
    top, rank_c = _topk_rows(cand, k)
    chosen = jnp.where(rank_c < float(k), 1.0, 0.0)
    z = jnp.zeros_like(top[0])
    for i in range(k):
        z = z + jnp.exp(top[i] - top[0])
    n1 = jnp.zeros(s1.shape, F32)
    for pp, r0, nr in spans:
        n_p = jnp.sum(chosen[r0:r0 + nr], axis=0, keepdims=True)
        n1 = jnp.where(rank1 == float(pp), n_p, n1)
    n1_ref[0] = n1
    w1_ref[0] = jnp.exp(s1 - v1[0]) / z
    r2_ref[0] = rank2
    w2_ref[0] = jnp.exp(s2 - v2[0])


def _route(qr, sub_keys, tt):
    n = qr.shape[0]
    heads, _, n_keys, d_half = sub_keys.shape
    keys = sub_keys.reshape(heads * 2, n_keys, d_half).astype(BF16)
    out = jax.ShapeDtypeStruct((heads, n_keys, n), F32)
    ospec = pl.BlockSpec((1, n_keys, tt), lambda i, h: (h, 0, i))
    return pl.pallas_call(
        _route_body,
        grid=(n // tt, heads),
        in_specs=[pl.BlockSpec((tt, 2 * d_half), lambda i, h: (i, h)),
                  pl.BlockSpec((2, n_keys, d_half), lambda i, h: (h, 0, 0))],
        out_specs=[ospec, ospec, ospec, ospec],
        out_shape=[out, out, out, out],
        compiler_params=_params("parallel", "arbitrary"),
        name="peer_route",
    )(qr, keys)


def _gelu(x):
    return 0.5 * x * (1.0 + lax.erf(x * (2.0 ** -0.5)))


def _peer_body(h_ref, u_ref, v_ref, n1_ref, w1_ref, r2_ref, w2_ref, x_ref, o_ref,
               z0_sc, z1_sc, a_sc, *, ec, tt, heads, n_keys):
    j = pl.program_id(1)
    nsl = ec // n_keys

    @pl.when(j == 0)
    def _init():
        o_ref[...] = x_ref[...]
        z1_sc[...] = jnp.zeros(z1_sc.shape, F32)

    def stage(z_next, z_prev):
        z_next[...] = lax.dot_general(u_ref[...], h_ref[...], (((1,), (1,)), ((), ())),
                                      preferred_element_type=F32)
        for sl in range(nsl):
            for tb in range(tt // LANES):
                cols = slice(tb * LANES, (tb + 1) * LANES)
                g = jnp.zeros((n_keys, LANES), F32)
                for h in range(heads):
                    row = h * nsl + sl
                    n1 = n1_ref[0, row:row + 1, cols]
                    w1 = w1_ref[0, row:row + 1, cols]
                    g = g + jnp.where(r2_ref[h, :, cols] < n1, w2_ref[h, :, cols] * w1, 0.0)
                z = z_prev[sl * n_keys:(sl + 1) * n_keys, cols]
                a_sc[sl * n_keys:(sl + 1) * n_keys, cols] = (_gelu(z) * g).astype(a_sc.dtype)
        o_ref[...] += lax.dot_general(a_sc[...], v_ref[...], (((0,), (0,)), ((), ())),
                                      preferred_element_type=F32)

    @pl.when(j % 2 == 0)
    def _even():
        stage(z0_sc, z1_sc)

    @pl.when(j % 2 == 1)
    def _odd():
        stage(z1_sc, z0_sc)


def _peer(h2, u_bf, v_bf, tables, x, tt, ec):
    n, d = h2.shape
    n_exp = u_bf.shape[0]
    n1, w1, r2, w2 = tables
    heads, n_keys, _ = n1.shape
    nsl = ec // n_keys
    nc = n_exp // ec

    def by_chunk(t):
        t = t.reshape(heads, n_keys // nsl, nsl, n)
        return jnp.transpose(t, (1, 0, 2, 3)).reshape(n_keys // nsl, heads * nsl, n)

    once = pl.Buffered(1)
    tspec = pl.BlockSpec((heads, n_keys, tt), lambda i, j: (0, 0, i), pipeline_mode=once)
    prev = lambda j: jnp.maximum(j - 1, 0)
    cspec = pl.BlockSpec((1, heads * nsl, tt), lambda i, j: (prev(j), 0, i))
    body = functools.partial(_peer_body, ec=ec, tt=tt, heads=heads, n_keys=n_keys)
    return pl.pallas_call(
        body,
        grid=(n // tt, nc + 1),
        in_specs=[pl.BlockSpec((tt, d), lambda i, j: (i, 0), pipeline_mode=once),
                  pl.BlockSpec((ec, d), lambda i, j: (jnp.minimum(j, nc - 1), 0)),
                  pl.BlockSpec((ec, d), lambda i, j: (prev(j), 0)),
                  cspec, cspec, tspec, tspec,
                  pl.BlockSpec((tt, d), lambda i, j: (i, 0), pipeline_mode=once)],
        out_specs=pl.BlockSpec((tt, d), lambda i, j: (i, 0)),
        out_shape=jax.ShapeDtypeStruct((n, d), F32),
        scratch_shapes=[pltpu.VMEM((ec, tt), F32), pltpu.VMEM((ec, tt), F32),
                        pltpu.VMEM((ec, tt), BF16)],
        compiler_params=_params("parallel", "arbitrary"),
        name="peer_experts",
    )(h2, u_bf, v_bf, by_chunk(n1), by_chunk(w1), r2, w2, x)


def _tile(n, pref):
    return pref if n % pref == 0 else n


def _project(x2d, ln_g, w_in_bf, qn_g, kn_g, widths):
    pool_w, attn_w = widths
    n, d = x2d.shape
    tm = _tile(n, 1024)
    xn = _rmsnorm(x2d, ln_g, _tile(n, 256))
    mm = functools.partial(_matmul, [xn], [(w_in_bf, 0, d)], tm=tm, tn=512, out_dtype=F32)
    u = mm(n_out=pool_w, w_col_off=0, name="proj_u")
    q = mm(n_out=attn_w, w_col_off=pool_w, gain=qn_g, qk_group=DQK, name="proj_q")
    k = mm(n_out=attn_w, w_col_off=pool_w + attn_w, gain=kn_g, qk_group=DQK, name="proj_k")
    v = mm(n_out=attn_w, w_col_off=pool_w + 2 * attn_w, name="proj_v")
    return u, q, k, v


def _mix_and_peer(x2d, pool_y, att, w_o_bf, ln2_g, w_query_bf, sub_keys, u_bf, v_bf):
    n, d = x2d.shape
    pool_w = pool_y.shape[1]
    tm = _tile(n, 1024)
    assert att.shape[1] == pool_w, "w_o row blocks assume equal pool and attention widths"
    x1 = _matmul([pool_y, att], [(w_o_bf, 0, pool_w), (w_o_bf, 1, pool_w)],
                 n_out=d, tm=tm, tn=512, out_dtype=F32, res=x2d, name="out_proj")
    h2 = _rmsnorm(x1, ln2_g, _tile(n, 256))
    qr = _matmul([h2], [(w_query_bf, 0, d)], n_out=w_query_bf.shape[1], tm=tm, tn=512,
                 out_dtype=BF16, name="peer_query")
    tables = _route(qr, sub_keys, _tile(n, LANES))
    return _peer(h2, u_bf, v_bf, tables, x1, _tile(n, 512), 512)


def kernel(x_prompt, x_sample, cache_k, cache_v, state_pool, page_table, rel_bias, ln1_g, w_in, q_norm_g, k_norm_g, lambda_q1, lambda_k1, lambda_q2, lambda_k2, subln_g, w_pool, pool_scale, w_o, ln2_g, w_query, sub_keys, peer_u, peer_v):
    batch, seq, d = x_prompt.shape
    bd, tq, _ = x_sample.shape
    assert tq == 1, "sample group is one new position per sequence"
    depth = w_in.shape[0]
    pool_w = w_pool.shape[1] * w_pool.shape[2]
    attn_w = (w_in.shape[2] - pool_w) // 3
    heads = attn_w // DV
    n_state = state_pool.shape[2]

    xp = x_prompt.reshape(batch * seq, d)
    xs = x_sample.reshape(bd, d)
    kp_l, vp_l, pp_l, ks_l, vs_l, ps_l = [], [], [], [], [], []
    for l in range(depth):
        lam_init = 0.8 - 0.6 * math.exp(-0.3 * l)
        lam_vecs = jnp.stack([lambda_q1[l], lambda_k1[l], lambda_q2[l], lambda_k2[l]]).astype(F32)
        w_in_bf = w_in[l].astype(BF16)
        w_o_bf = w_o[l].astype(BF16)
        w_query_bf = w_query[l].astype(BF16)
        u_bf = peer_u[l].astype(BF16)
        v_bf = peer_v[l].astype(BF16)

        u, q, k, v = _project(xp, ln1_g[l], w_in_bf, q_norm_g[l], k_norm_g[l], (pool_w, attn_w))
        pool_y = _pool_prompt(u, w_pool[l], pool_scale[l], batch, seq)
        att = _attn_prompt(q, k, v, lam_vecs, rel_bias, subln_g[l], batch, seq, lam_init)
        xp = _mix_and_peer(xp, pool_y, att, w_o_bf, ln2_g[l], w_query_bf, sub_keys[l], u_bf, v_bf)
        kp_l.append(k.reshape(batch, seq, heads, 2 * DQK))
        vp_l.append(v.reshape(batch, seq, heads, DV))
        pp_l.append(u.reshape(batch, seq, pool_w)[:, seq - n_state:])

        u, q, k, v = _project(xs, ln1_g[l], w_in_bf, q_norm_g[l], k_norm_g[l], (pool_w, attn_w))
        pool_y = _pool_sample(state_pool[l], u, w_pool[l], pool_scale[l])
        att = _attn_decode(q, k, v, cache_k, cache_v, l, page_table, lam_vecs, rel_bias,
                           subln_g[l], lam_init).astype(BF16)
        xs = _mix_and_peer(xs, pool_y, att, w_o_bf, ln2_g[l], w_query_bf, sub_keys[l], u_bf, v_bf)
        ks_l.append(k.reshape(bd, tq, heads, 2 * DQK))
        vs_l.append(v.reshape(bd, tq, heads, DV))
        ps_l.append(jnp.concatenate([state_pool[l], u[:, None, :]], axis=1)[:, -n_state:])

    return (xp.reshape(batch, seq, d), xs.reshape(bd, tq, d),
            jnp.stack(kp_l), jnp.stack(vp_l), jnp.stack(pp_l),
            jnp.stack(ks_l), jnp.stack(vs_l), jnp.stack(ps_l))
```

```python
import functools
import math

import jax
import jax.numpy as jnp
from jax import lax
from jax.experimental import pallas as pl
from jax.experimental.pallas import tpu as pltpu

F32 = jnp.float32
BF16 = jnp.bfloat16
EPS = 1e-6

LANES = 128
VMEM_LIMIT_BYTES = 56 * 1024 * 1024

POOL_WINDOWS = (2, 4, 8, 16)
DQK = 64
DV = 128
NUM_BUCKETS = 32
MAX_DISTANCE = 128
PEER_TOPK = 16
PAGE_SIZE = 128


def _params(*sem, flags=None):
    return pltpu.CompilerParams(dimension_semantics=sem, vmem_limit_bytes=VMEM_LIMIT_BYTES,
                                flags=flags)


def _split_bf16(x):
    hi = x.astype(BF16)
    lo = (x - hi.astype(F32)).astype(BF16)
    return hi, lo


def _dot(a, b):
    return jnp.dot(a, b, preferred_element_type=F32)


def _dot2(x, w):
    hi, lo = _split_bf16(x)
    return _dot(hi, w) + _dot(lo, w)


def _rmsnorm_body(x_ref, g_ref, o_ref):
    x = x_ref[...]
    ms = jnp.mean(x * x, axis=-1, keepdims=True)
    o_ref[...] = (x * lax.rsqrt(ms + EPS) * g_ref[...]).astype(o_ref.dtype)


def _rmsnorm(x, g, tm):
    m, d = x.shape
    return pl.pallas_call(
        _rmsnorm_body,
        grid=(m // tm,),
        in_specs=[pl.BlockSpec((tm, d), lambda i: (i, 0)),
                  pl.BlockSpec((1, d), lambda i: (0, 0))],
        out_specs=pl.BlockSpec((tm, d), lambda i: (i, 0)),
        out_shape=jax.ShapeDtypeStruct((m, d), BF16),
        compiler_params=_params("parallel"),
        name="rmsnorm",
    )(x, g.reshape(1, d))


def _mm_body(*refs, n_pairs, qk_group, has_res):
    a_refs = refs[:n_pairs]
    w_refs = refs[n_pairs:2 * n_pairs]
    pos = 2 * n_pairs
    if qk_group:
        g_ref, ones_ref = refs[pos], refs[pos + 1]
        pos += 2
    if has_res:
        r_ref = refs[pos]
        pos += 1
    o_ref = refs[pos]

    acc = _dot(a_refs[0][...], w_refs[0][...])
    for p in range(1, n_pairs):
        acc = acc + _dot(a_refs[p][...], w_refs[p][...])
    if qk_group:
        sq = acc * acc
        hi, lo = _split_bf16(sq)
        ones = ones_ref[...]
        parts = []
        for s in range(acc.shape[1] // LANES):
            sl = slice(s * LANES, (s + 1) * LANES)
            parts.append(_dot(hi[:, sl], ones) + _dot(lo[:, sl], ones))
        gs = jnp.concatenate(parts, axis=1)
        acc = acc * lax.rsqrt(gs * (1.0 / qk_group) + EPS) * g_ref[...]
    if has_res:
        acc = acc + r_ref[...]
    o_ref[...] = acc.astype(o_ref.dtype)


def _matmul(a_list, w_list, *, n_out, tm, tn, out_dtype, w_col_off=0,
            gain=None, qk_group=0, res=None, name="matmul"):
    m = a_list[0].shape[0]
    tn = _tile(n_out, tn)
    off = w_col_off // tn
    in_specs, args = [], []
    for a in a_list:
        in_specs.append(pl.BlockSpec((tm, a.shape[1]), lambda i, j: (i, 0)))
        args.append(a)
    for w, rb, kr in w_list:
        in_specs.append(pl.BlockSpec((kr, tn), lambda i, j, rb=rb: (rb, j + off)))
        args.append(w)
    if qk_group:
        lane = jnp.arange(LANES)
        ones_bd = (lane[:, None] // qk_group == lane[None, :] // qk_group).astype(BF16)
        in_specs += [pl.BlockSpec((1, tn), lambda i, j: (0, 0)),
                     pl.BlockSpec((LANES, LANES), lambda i, j: (0, 0))]
        args += [jnp.tile(gain.astype(F32), tn // gain.shape[0]).reshape(1, tn), ones_bd]
    if res is not None:
        in_specs.append(pl.BlockSpec((tm, tn), lambda i, j: (i, j)))
        args.append(res)
    body = functools.partial(_mm_body, n_pairs=len(a_list), qk_group=qk_group,
                             has_res=res is not None)
    return pl.pallas_call(
        body,
        grid=(m // tm, n_out // tn),
        in_specs=in_specs,
        out_specs=pl.BlockSpec((tm, tn), lambda i, j: (i, j)),
        out_shape=jax.ShapeDtypeStruct((m, n_out), out_dtype),
        compiler_params=_params("parallel", "arbitrary"),
        name=name,
    )(*args)


_POOL_PAD = 16
_POOL_ROWS = 256


def _pool_prompt_body(u_ref, w_ref, sc_ref, o_ref, pad_ref, *, seq):
    g = pl.program_id(1)
    pad_ref[0:_POOL_PAD, :] = jnp.zeros((_POOL_PAD, pad_ref.shape[1]), F32)
    pad_ref[_POOL_PAD:, :] = u_ref[...]
    wmat = w_ref[0].astype(BF16)
    scale = sc_ref[...]
    for gi, win in enumerate(POOL_WINDOWS):
        @pl.when(g == gi)
        def _(win=win):
            for r0 in range(0, seq, _POOL_ROWS):
                cur = pad_ref[r0 + _POOL_PAD:r0 + _POOL_PAD + _POOL_ROWS, :]
                tot = cur
                for i in range(1, win):
                    tot = tot + pad_ref[r0 + _POOL_PAD - i:r0 + _POOL_PAD - i + _POOL_ROWS, :]
                pos = r0 + lax.broadcasted_iota(jnp.int32, (_POOL_ROWS, 1), 0)
                cnt = jnp.minimum(win, pos + 1).astype(F32)
                d = tot / cnt - cur
                y = _dot(d.astype(BF16), wmat) * scale
                o_ref[r0:r0 + _POOL_ROWS, :] = y.astype(o_ref.dtype)


def _pool_prompt(u, w_pool, pool_scale, batch, seq):
    n, width = u.shape
    ng = len(POOL_WINDOWS)
    gw = width // ng
    body = functools.partial(_pool_prompt_body, seq=seq)
    return pl.pallas_call(
        body,
        grid=(batch, ng),
        in_specs=[pl.BlockSpec((seq, gw), lambda b, g: (b, g)),
                  pl.BlockSpec((1, gw, gw), lambda b, g: (g, 0, 0)),
                  pl.BlockSpec((1, gw), lambda b, g: (0, g))],
        out_specs=pl.BlockSpec((seq, gw), lambda b, g: (b, g)),
        out_shape=jax.ShapeDtypeStruct((n, width), BF16),
        scratch_shapes=[pltpu.VMEM((seq + _POOL_PAD, gw), F32)],
        compiler_params=_params("parallel", "arbitrary"),
        name="pool_prompt",
    )(u, w_pool, pool_scale.reshape(1, width))


def _pool_sample_body(st_ref, u_ref, w_ref, sc_ref, o_ref, *, n_state, width):
    ng = len(POOL_WINDOWS)
    gw = width // ng
    for gi, win in enumerate(POOL_WINDOWS):
        c0 = gi * gw
        cur = u_ref[:, c0:c0 + gw]
        tot = cur
        for i in range(1, win):
            r = n_state - i
            tot = tot + st_ref[:, r * width + c0:r * width + c0 + gw]
        d = tot / float(win) - cur
        y = _dot(d.astype(BF16), w_ref[gi].astype(BF16)) * sc_ref[:, c0:c0 + gw]
        o_ref[:, c0:c0 + gw] = y.astype(o_ref.dtype)


def _pool_sample(state, u, w_pool, pool_scale):
    bd, n_state, width = state.shape
    body = functools.partial(_pool_sample_body, n_state=n_state, width=width)
    return pl.pallas_call(
        body,
        out_shape=jax.ShapeDtypeStruct((bd, width), BF16),
        compiler_params=pltpu.CompilerParams(vmem_limit_bytes=VMEM_LIMIT_BYTES),
        name="pool_sample",
    )(state.reshape(bd, n_state * width), u, w_pool, pool_scale.reshape(1, width))


def _rel_bucket(n):
    max_exact = NUM_BUCKETS // 2
    nf = jnp.maximum(n, 1).astype(F32)
    large = max_exact + (jnp.log(nf / max_exact) / math.log(MAX_DISTANCE / max_exact)
                         * (NUM_BUCKETS - max_exact)).astype(jnp.int32)
    large = jnp.minimum(large, NUM_BUCKETS - 1)
    return jnp.where(n < max_exact, n, large)


def _bias_by_distance(rel_bias, n_dist):
    return rel_bias.astype(F32)[_rel_bucket(jnp.arange(n_dist))]


def _lambda(lv_ref, lam_init):
    lv = lv_ref[...]
    a = jnp.sum(lv[0:1] * lv[1:2], axis=-1, keepdims=True)
    b = jnp.sum(lv[2:3] * lv[3:4], axis=-1, keepdims=True)
    return jnp.exp(a) - jnp.exp(b) + lam_init


def _attn_prompt_body(lv_ref, q_ref, k_ref, v_ref, bias_ref, g_ref, o_ref,
                      vt_sc, m_sc, l_sc, acc_sc, *, tq, tk, seq, sb, lam_init):
    qi = pl.program_id(2)

    @pl.when(qi == 0)
    def _transpose_values():
        for c in range(seq // tk):
            vt_sc[c] = v_ref[c * tk:(c + 1) * tk, :].T.astype(BF16)

    q = q_ref[...] * (DQK ** -0.5)
    lane = lax.broadcasted_iota(jnp.int32, q.shape, 1)
    q_ext = jnp.concatenate([jnp.where(lane < DQK, q, 0.0),
                             jnp.where(lane >= DQK, q, 0.0)], axis=0).astype(BF16)
    m_sc[...] = jnp.full(m_sc.shape, -jnp.inf, F32)
    l_sc[...] = jnp.zeros(l_sc.shape, F32)
    acc_sc[...] = jnp.zeros(acc_sc.shape, F32)

    def step(j, carry):
        r0 = pl.multiple_of(j * tk, tk)
        kj = k_ref[pl.ds(r0, tk), :].astype(BF16)
        st = lax.dot_general(kj, q_ext, (((1,), (1,)), ((), ())),
                             preferred_element_type=F32)
        d0 = (qi * tq - j * tk) // sb
        rows = []
        for a in range(tk // sb):
            row = [bias_ref[0, jnp.clip(d0 + b - a + 1, 0, 3)] for b in range(tq // sb)]
            rows.append(jnp.concatenate(row + row, axis=1))
        st = st + jnp.concatenate(rows, axis=0)
        m_old = m_sc[...]
        m_new = jnp.maximum(m_old, jnp.max(st, axis=0, keepdims=True))
        alpha = jnp.exp(m_old - m_new)
        p = jnp.exp(st - m_new)
        l_sc[...] = alpha * l_sc[...] + jnp.sum(p, axis=0, keepdims=True)
        acc_sc[...] = alpha * acc_sc[...] + _dot(vt_sc[j], p.astype(BF16))
        m_sc[...] = m_new
        return carry

    lax.fori_loop(0, (qi * tq) // tk + 1, step, 0)

    lam = _lambda(lv_ref, lam_init)
    o = acc_sc[...] / l_sc[...]
    att = (o[:, :tq] - lam * o[:, tq:]).T
    ms = jnp.mean(att * att, axis=-1, keepdims=True)
    out = att * lax.rsqrt(ms + EPS) * g_ref[...] * (1.0 - lam_init)
    o_ref[...] = out.astype(o_ref.dtype)


def _toeplitz(g, rows, cols):
    h = g.shape[0]
    period = rows + cols
    gp = jnp.pad(g, ((0, 0), (0, 1)))
    flat = jnp.tile(gp, (1, rows))[:, :rows * (period - 1)]
    return flat.reshape(h, rows, period - 1)[:, :, rows - 1:rows - 1 + cols]


def _attn_prompt(q, k, v, lam_vecs, rel_bias, subln_g, batch, seq, lam_init, tq=512, tk=512):
    n, width = q.shape
    heads = width // DV
    nq = seq // tq
    sb = MAX_DISTANCE
    assert seq % tk == 0 and tk % tq == 0 and tq % sb == 0 and sb == LANES
    bd = _bias_by_distance(rel_bias, 2 * sb).T
    line = jnp.concatenate([jnp.full((heads, sb - 1), -jnp.inf, F32), bd], axis=1)
    blocks = [jnp.full((heads, sb, sb), -jnp.inf, F32)]
    blocks += [_toeplitz(line[:, d * sb:d * sb + 2 * sb - 1], sb, sb) for d in range(2)]
    blocks.append(jnp.broadcast_to(bd[:, 2 * sb - 1][:, None, None], (heads, sb, sb)))
    bias_tiles = jnp.stack(blocks, axis=1)
    body = functools.partial(_attn_prompt_body, tq=tq, tk=tk, seq=seq, sb=sb,
                             lam_init=lam_init)
    return pl.pallas_call(
        body,
        grid=(batch, heads, nq),
        in_specs=[pl.BlockSpec((4, DQK), lambda b, h, i: (0, 0)),
                  pl.BlockSpec((tq, DV), lambda b, h, i: (b * nq + i, h)),
                  pl.BlockSpec((seq, DV), lambda b, h, i: (b, h)),
                  pl.BlockSpec((seq, DV), lambda b, h, i: (b, h)),
                  pl.BlockSpec((1, 4, sb, sb), lambda b, h, i: (h, 0, 0, 0)),
                  pl.BlockSpec((1, DV), lambda b, h, i: (0, 0))],
        out_specs=pl.BlockSpec((tq, DV), lambda b, h, i: (b * nq + i, h)),
        out_shape=jax.ShapeDtypeStruct((n, width), BF16),
        scratch_shapes=[pltpu.VMEM((seq // tk, DV, tk), BF16),
                        pltpu.VMEM((1, 2 * tq), F32), pltpu.VMEM((1, 2 * tq), F32),
                        pltpu.VMEM((DV, 2 * tq), F32)],
        compiler_params=_params("parallel", "parallel", "arbitrary"),
        name="attn_prompt",
    )(lam_vecs, q, k, v, bias_tiles, subln_g.reshape(1, DV))


_PAGES_PER_STEP = 8


def _attn_decode_body(pt_ref, lv_ref, q_ref, kn_ref, vn_ref, bias_ref, bias0_ref, g_ref,
                      *rest, n_pages, heads, lam_init):
    pp = _PAGES_PER_STEP
    k_refs = rest[:pp]
    v_refs = rest[pp:2 * pp]
    o_ref = rest[2 * pp]
    p_sc, m_sc, l_sc, f_sc, wn_sc, acc_sc = rest[2 * pp + 1:]
    del pt_ref
    nk = n_pages // pp
    s = pl.program_id(1)
    q = q_ref[0] * (DQK ** -0.5)
    lane = lax.broadcasted_iota(jnp.int32, q.shape, 1)
    q_all = jnp.concatenate([jnp.where(lane < DQK, q, 0.0),
                             jnp.where(lane >= DQK, q, 0.0)], axis=0)
    stat = m_sc.shape[1:]

    @pl.when(s < nk)
    def _scores():
        qb = q_all.astype(BF16)
        for r in range(pp):
            pg = s * pp + r
            sc = lax.dot_general(qb, k_refs[r][0].astype(BF16), (((1,), (1,)), ((), ())),
                                 preferred_element_type=F32)
            sc = sc + bias_ref[pg]
            m_pg = jnp.max(sc, axis=1, keepdims=True)
            p = jnp.exp(sc - m_pg)
            p_sc[pg] = p
            m_sc[pg] = jnp.broadcast_to(m_pg, stat)
            l_sc[pg] = jnp.broadcast_to(jnp.sum(p, axis=1, keepdims=True), stat)

    @pl.when(s == nk - 1)
    def _merge():
        kn = kn_ref[0]
        sn = (jnp.sum(q_all * jnp.concatenate([kn, kn], axis=0), axis=1, keepdims=True)
              + bias0_ref[...])
        m_all = m_sc[...]
        m = jnp.maximum(jnp.max(m_all, axis=0), sn)
        e = jnp.exp(m_all - m[None])
        pn = jnp.exp(sn - m)
        l = jnp.sum(l_sc[...] * e, axis=0) + pn
        lam = _lambda(lv_ref, lam_init)
        r = jnp.concatenate([1.0 / l[:heads], lam / l[heads:]], axis=0)
        f_sc[...] = e * r[None]
        wn = pn * r
        wn_sc[...] = wn[:heads] - wn[heads:]
        acc_sc[...] = jnp.zeros(acc_sc.shape, F32)

    @pl.when(s >= nk)
    def _values():
        tot = acc_sc[...]
        reps = p_sc.shape[2] // LANES
        for r in range(pp):
            pg = (s - nk) * pp + r
            pw = p_sc[pg] * jnp.tile(f_sc[pg], (1, reps))
            w = (pw[:heads] - pw[heads:]).astype(BF16)
            tot = tot + _dot(w, v_refs[r][0].astype(BF16))
        acc_sc[...] = tot

    @pl.when(s == 2 * nk - 1)
    def _finish():
        o = acc_sc[...] + wn_sc[...] * vn_ref[0]
        ms = jnp.mean(o * o, axis=-1, keepdims=True)
        o_ref[0] = o * lax.rsqrt(ms + EPS) * g_ref[...] * (1.0 - lam_init)


def _attn_decode(q, k_new, v_new, cache_k, cache_v, layer, page_table, lam_vecs, rel_bias,
                 subln_g, lam_init):
    bd, width = q.shape
    heads = width // DV
    n_pages = page_table.shape[1]
    past = n_pages * PAGE_SIZE
    pp = _PAGES_PER_STEP
    nk = n_pages // pp
    depth, n_pool = cache_k.shape[:2]
    ck = cache_k.reshape(depth * n_pool, PAGE_SIZE * heads, DV)
    cv = cache_v.reshape(depth * n_pool, PAGE_SIZE * heads, DV)
    page0 = layer * n_pool
    bdist = _bias_by_distance(rel_bias, past + 1)
    b_past = bdist[1:][::-1].reshape(n_pages, 1, PAGE_SIZE, heads)
    own_head = jnp.eye(heads, dtype=bool)[None, :, None, :]
    b_past = jnp.where(own_head, b_past, -jnp.inf).reshape(n_pages, heads, PAGE_SIZE * heads)
    b_past = jnp.concatenate([b_past, b_past], axis=1)
    b_new = jnp.broadcast_to(jnp.tile(bdist[0], 2)[:, None], (2 * heads, LANES))

    def k_map(r):
        return lambda b, s, pt: (page0 + pt[b, jnp.minimum(s, nk - 1) * pp + r], 0, 0)

    def v_map(r):
        return lambda b, s, pt: (page0 + pt[b, jnp.maximum(s - nk, 0) * pp + r], 0, 0)

    row3 = lambda b, s, pt: (b, 0, 0)
    const2 = lambda b, s, pt: (0, 0)
    page_rows = PAGE_SIZE * heads
    in_specs = [pl.BlockSpec((4, DQK), const2),
                pl.BlockSpec((1, heads, DV), row3),
                pl.BlockSpec((1, heads, DV), row3),
                pl.BlockSpec((1, heads, DV), row3),
                pl.BlockSpec((n_pages, 2 * heads, page_rows), lambda b, s, pt: (0, 0, 0),
                             pipeline_mode=pl.Buffered(1)),
                pl.BlockSpec((2 * heads, LANES), const2),
                pl.BlockSpec((1, DV), const2)]
    in_specs += [pl.BlockSpec((1, page_rows, DV), k_map(r)) for r in range(pp)]
    in_specs += [pl.BlockSpec((1, page_rows, DV), v_map(r)) for r in range(pp)]
    body = functools.partial(_attn_decode_body, n_pages=n_pages, heads=heads, lam_init=lam_init)
    out = pl.pallas_call(
        body,
        grid_spec=pltpu.PrefetchScalarGridSpec(
            num_scalar_prefetch=1,
            grid=(bd, 2 * nk),
            in_specs=in_specs,
            out_specs=pl.BlockSpec((1, heads, DV), row3),
            scratch_shapes=[pltpu.VMEM((n_pages, 2 * heads, page_rows), F32),
                            pltpu.VMEM((n_pages, 2 * heads, LANES), F32),
                            pltpu.VMEM((n_pages, 2 * heads, LANES), F32),
                            pltpu.VMEM((n_pages, 2 * heads, LANES), F32),
                            pltpu.VMEM((heads, LANES), F32),
                            pltpu.VMEM((heads, DV), F32)]),
        out_shape=jax.ShapeDtypeStruct((bd, heads, DV), F32),
        compiler_params=_params("parallel", "arbitrary"),
        name="attn_decode",
    )(page_table, lam_vecs, q.reshape(bd, heads, DV), k_new.reshape(bd, heads, DV),
      v_new.reshape(bd, heads, DV), b_past, b_new, subln_g.reshape(1, DV),
      *([ck] * pp), *([cv] * pp))
    return out.reshape(bd, width)


def _topk_rows(s, k, ties):
    rows = s.shape[0]
    rank = jnp.full(s.shape, float(k), F32)
    if ties:
        iota = lax.broadcasted_iota(jnp.int32, s.shape, 0).astype(F32)
    vals = []
    for p in range(k):
        m = jnp.max(s, axis=0, keepdims=True)
        sel = s == m
        if ties:
            idx = jnp.min(jnp.where(sel, iota, float(rows)), axis=0, keepdims=True)
            sel = iota == idx
        rank = jnp.where(sel, float(p), rank)
        s = jnp.where(sel, -jnp.inf, s)
        vals.append(m)
    return vals, rank


def _route_tables(s1, s2, ties):
    k = PEER_TOPK
    v1, rank1 = _topk_rows(s1, k, ties)
    v2, rank2 = _topk_rows(s2, k, ties)
    v1_all = jnp.concatenate(v1, axis=0)
    v2_all = jnp.concatenate(v2, axis=0)
    sub = lax.broadcasted_iota(jnp.int32, (8, s1.shape[1]), 0)
    blocks, spans, row = [], [], 0
    p = 0
    while p < k:
        nq = k // (p + 1)
        if nq >= 8:
            nb = -(-nq // 8) * 8
            blk = v1[p] + v2_all[0:nb]
            if nb != nq:
                blk = jnp.where(jnp.concatenate([sub + 8 * i for i in range(nb // 8)]) < nq,
                                blk, -jnp.inf)
            blocks.append(blk)
            spans.append((p, row, nb))
            row += nb
            p += 1
        elif nq > 1:
            blocks.append(jnp.where(sub < nq, v1[p] + v2_all[0:8], -jnp.inf))
            spans.append((p, row, 8))
            row += 8
            p += 1
        else:
            assert (k - p) == 8, "single-candidate rows are packed as one aligned 8-row block"
            blocks.append(v1_all[p:p + 8] + v2[0])
            for r in range(8):
                spans.append((p + r, row + r, 1))
            row += 8
            p += 8
    cand = jnp.concatenate(blocks, axis=0)
    top, rank_c = _topk_rows(cand, k, ties)
    chosen = jnp.where(rank_c < float(k), 1.0, 0.0)
    z = jnp.zeros_like(top[0])
    for i in range(k):
        z = z + jnp.exp(top[i] - top[0])
    n1 = jnp.zeros(s1.shape, F32)
    for pp, r0, nr in spans:
        n_p = jnp.sum(chosen[r0:r0 + nr], axis=0, keepdims=True)
        n1 = jnp.where(rank1 == float(pp), n_p, n1)
    marked = (jnp.sum(jnp.where(rank1 < float(k), 1.0, 0.0), axis=0, keepdims=True)
              + jnp.sum(jnp.where(rank2 < float(k), 1.0, 0.0), axis=0, keepdims=True)
              + jnp.sum(chosen, axis=0, keepdims=True))
    tables = (n1, jnp.exp(s1 - v1[0]) / z, rank2, jnp.exp(s2 - v2[0]))
    return tables, marked - 3.0 * k


def _route_body(qr_ref, keys_ref, n1_ref, w1_ref, r2_ref, w2_ref):
    nt = (((1,), (1,)), ((), ()))
    s1 = lax.dot_general(keys_ref[0], qr_ref[:, 0:LANES], nt, preferred_element_type=F32)
    s2 = lax.dot_general(keys_ref[1], qr_ref[:, LANES:2 * LANES], nt, preferred_element_type=F32)
    out_refs = (n1_ref, w1_ref, r2_ref, w2_ref)
    tables, extra = _route_tables(s1, s2, ties=False)
    tied = jnp.max(extra) > 0.0
    for ref, t in zip(out_refs, tables):
        ref[0] = t

    @pl.when(tied)
    def _exact():
        exact_tables, _ = _route_tables(s1, s2, ties=True)
        for ref, t in zip(out_refs, exact_tables):
            ref[0] = t


def _route(qr, sub_keys, tt):
    n = qr.shape[0]
    heads, _, n_keys, d_half = sub_keys.shape
    keys = sub_keys.reshape(heads * 2, n_keys, d_half).astype(BF16)
    out = jax.ShapeDtypeStruct((heads, n_keys, n), F32)
    ospec = pl.BlockSpec((1, n_keys, tt), lambda i, h: (h, 0, i))
    return pl.pallas_call(
        _route_body,
        grid=(n // tt, heads),
        in_specs=[pl.BlockSpec((tt, 2 * d_half), lambda i, h: (i, h)),
                  pl.BlockSpec((2, n_keys, d_half), lambda i, h: (h, 0, 0))],
        out_specs=[ospec, ospec, ospec, ospec],
        out_shape=[out, out, out, out],
        compiler_params=_params("parallel", "arbitrary"),
        name="peer_route",
    )(qr, keys)


def _gelu(x):
    return 0.5 * x * (1.0 + lax.erf(x * (2.0 ** -0.5)))


def _peer_body(h_ref, u_ref, v_ref, n1_ref, w1_ref, r2_ref, w2_ref, x_ref, o_ref,
               z0_sc, z1_sc, a_sc, *, ec, tt, heads, n_keys):
    j = pl.program_id(1)
    nsl = ec // n_keys

    @pl.when(j == 0)
    def _init():
        o_ref[...] = x_ref[...]
        z1_sc[...] = jnp.zeros(z1_sc.shape, F32)

    def stage(z_next, z_prev):
        z_next[...] = lax.dot_general(u_ref[...], h_ref[...], (((1,), (1,)), ((), ())),
                                      preferred_element_type=F32)
        for sl in range(nsl):
            for tb in range(tt // LANES):
                cols = slice(tb * LANES, (tb + 1) * LANES)
                g = jnp.zeros((n_keys, LANES), F32)
                for h in range(heads):
                    row = h * nsl + sl
                    n1 = n1_ref[0, row:row + 1, cols]
                    w1 = w1_ref[0, row:row + 1, cols]
                    g = g + jnp.where(r2_ref[h, :, cols] < n1, w2_ref[h, :, cols] * w1, 0.0)
                z = z_prev[sl * n_keys:(sl + 1) * n_keys, cols]
                a_sc[sl * n_keys:(sl + 1) * n_keys, cols] = (_gelu(z) * g).astype(a_sc.dtype)
        o_ref[...] += lax.dot_general(a_sc[...], v_ref[...], (((0,), (0,)), ((), ())),
                                      preferred_element_type=F32)

    @pl.when(j % 2 == 0)
    def _even():
        stage(z0_sc, z1_sc)

    @pl.when(j % 2 == 1)
    def _odd():
        stage(z1_sc, z0_sc)


def _peer(h2, u_bf, v_bf, tables, x, tt, ec):
    n, d = h2.shape
    n_exp = u_bf.shape[0]
    n1, w1, r2, w2 = tables
    heads, n_keys, _ = n1.shape
    nsl = ec // n_keys
    nc = n_exp // ec

    def by_chunk(t):
        t = t.reshape(heads, n_keys // nsl, nsl, n)
        return jnp.transpose(t, (1, 0, 2, 3)).reshape(n_keys // nsl, heads * nsl, n)

    once = pl.Buffered(1)
    tspec = pl.BlockSpec((heads, n_keys, tt), lambda i, j: (0, 0, i), pipeline_mode=once)
    prev = lambda j: jnp.maximum(j - 1, 0)
    cspec = pl.BlockSpec((1, heads * nsl, tt), lambda i, j: (prev(j), 0, i))
    body = functools.partial(_peer_body, ec=ec, tt=tt, heads=heads, n_keys=n_keys)
    return pl.pallas_call(
        body,
        grid=(n // tt, nc + 1),
        in_specs=[pl.BlockSpec((tt, d), lambda i, j: (i, 0), pipeline_mode=once),
                  pl.BlockSpec((ec, d), lambda i, j: (jnp.minimum(j, nc - 1), 0)),
                  pl.BlockSpec((ec, d), lambda i, j: (prev(j), 0)),
                  cspec, cspec, tspec, tspec,
                  pl.BlockSpec((tt, d), lambda i, j: (i, 0), pipeline_mode=once)],
        out_specs=pl.BlockSpec((tt, d), lambda i, j: (i, 0)),
        out_shape=jax.ShapeDtypeStruct((n, d), F32),
        scratch_shapes=[pltpu.VMEM((ec, tt), F32), pltpu.VMEM((ec, tt), F32),
                        pltpu.VMEM((ec, tt), BF16)],
        compiler_params=_params("parallel", "arbitrary"),
        name="peer_experts",
    )(h2, u_bf, v_bf, by_chunk(n1), by_chunk(w1), r2, w2, x)


def _tile(n, pref):
    return pref if n % pref == 0 else n


def _project(x2d, ln_g, w_in_bf, qn_g, kn_g, widths):
    pool_w, attn_w = widths
    n, d = x2d.shape
    tm = _tile(n, 1024)
    xn = _rmsnorm(x2d, ln_g, _tile(n, 256))
    mm = functools.partial(_matmul, [xn], [(w_in_bf, 0, d)], tm=tm, tn=512, out_dtype=F32)
    u = mm(n_out=pool_w, w_col_off=0, name="proj_u")
    q = mm(n_out=attn_w, w_col_off=pool_w, gain=qn_g, qk_group=DQK, name="proj_q")
    k = mm(n_out=attn_w, w_col_off=pool_w + attn_w, gain=kn_g, qk_group=DQK, name="proj_k")
    v = mm(n_out=attn_w, w_col_off=pool_w + 2 * attn_w, name="proj_v")
    return u, q, k, v


def _mix_and_peer(x2d, pool_y, att, w_o_bf, ln2_g, w_query_bf, sub_keys, u_bf, v_bf):
    n, d = x2d.shape
    pool_w = pool_y.shape[1]
    tm = _tile(n, 1024)
    assert att.shape[1] == pool_w, "w_o row blocks assume equal pool and attention widths"
    x1 = _matmul([pool_y, att], [(w_o_bf, 0, pool_w), (w_o_bf, 1, pool_w)],
                 n_out=d, tm=tm, tn=512, out_dtype=F32, res=x2d, name="out_proj")
    h2 = _rmsnorm(x1, ln2_g, _tile(n, 256))
    qr = _matmul([h2], [(w_query_bf, 0, d)], n_out=w_query_bf.shape[1], tm=tm, tn=512,
                 out_dtype=BF16, name="peer_query")
    tables = _route(qr, sub_keys, _tile(n, LANES))
    return _peer(h2, u_bf, v_bf, tables, x1, _tile(n, 512), 512)


def kernel(x_prompt, x_sample, cache_k, cache_v, state_pool, page_table, rel_bias, ln1_g, w_in, q_norm_g, k_norm_g, lambda_q1, lambda_k1, lambda_q2, lambda_k2, subln_g, w_pool, pool_scale, w_o, ln2_g, w_query, sub_keys, peer_u, peer_v):
    batch, seq, d = x_prompt.shape
    bd, tq, _ = x_sample.shape
    assert tq == 1, "sample group is one new position per sequence"
    depth = w_in.shape[0]
    pool_w = w_pool.shape[1] * w_pool.shape[2]
    attn_w = (w_in.shape[2] - pool_w) // 3
    heads = attn_w // DV
    n_state = state_pool.shape[2]

    xp = x_prompt.reshape(batch * seq, d)
    xs = x_sample.reshape(bd, d)
    kp_l, vp_l, pp_l, ks_l, vs_l, ps_l = [], [], [], [], [], []
    for l in range(depth):
        lam_init = 0.8 - 0.6 * math.exp(-0.3 * l)
        lam_vecs = jnp.stack([lambda_q1[l], lambda_k1[l], lambda_q2[l], lambda_k2[l]]).astype(F32)
        w_in_bf = w_in[l].astype(BF16)
        w_o_bf = w_o[l].astype(BF16)
        w_query_bf = w_query[l].astype(BF16)
        u_bf = peer_u[l].astype(BF16)
        v_bf = peer_v[l].astype(BF16)

        u, q, k, v = _project(xp, ln1_g[l], w_in_bf, q_norm_g[l], k_norm_g[l], (pool_w, attn_w))
        pool_y = _pool_prompt(u, w_pool[l], pool_scale[l], batch, seq)
        att = _attn_prompt(q, k, v, lam_vecs, rel_bias, subln_g[l], batch, seq, lam_init)
        xp = _mix_and_peer(xp, pool_y, att, w_o_bf, ln2_g[l], w_query_bf, sub_keys[l], u_bf, v_bf)
        kp_l.append(k.reshape(batch, seq, heads, 2 * DQK))
        vp_l.append(v.reshape(batch, seq, heads, DV))
        pp_l.append(u.reshape(batch, seq, pool_w)[:, seq - n_state:])

        u, q, k, v = _project(xs, ln1_g[l], w_in_bf, q_norm_g[l], k_norm_g[l], (pool_w, attn_w))
        pool_y = _pool_sample(state_pool[l], u, w_pool[l], pool_scale[l])
        att = _attn_decode(q, k, v, cache_k, cache_v, l, page_table, lam_vecs, rel_bias,
                           subln_g[l], lam_init).astype(BF16)
        xs = _mix_and_peer(xs, pool_y, att, w_o_bf, ln2_g[l], w_query_bf, sub_keys[l], u_bf, v_bf)
        ks_l.append(k.reshape(bd, tq, heads, 2 * DQK))
        vs_l.append(v.reshape(bd, tq, heads, DV))
        ps_l.append(jnp.concatenate([state_pool[l], u[:, None, :]], axis=1)[:, -n_state:])

    return (xp.reshape(batch, seq, d), xs.reshape(bd, tq, d),
            jnp.stack(kp_l), jnp.stack(vp_l), jnp.stack(pp_l),
            jnp.stack(ks_l), jnp.stack(vs_l), jnp.stack(ps_l))
```

```python
import functools
import math

import jax
import jax.numpy as jnp
from jax import lax
from jax.experimental import pallas as pl
from jax.experimental.pallas import tpu as pltpu

F32 = jnp.float32
BF16 = jnp.bfloat16
EPS = 1e-6

LANES = 128
VMEM_LIMIT_BYTES = 56 * 1024 * 1024

POOL_WINDOWS = (2, 4, 8, 16)
DQK = 64
DV = 128
NUM_BUCKETS = 32
MAX_DISTANCE = 128
PEER_TOPK = 16
PAGE_SIZE = 128


def _params(*sem, flags=None):
    return pltpu.CompilerParams(dimension_semantics=sem, vmem_limit_bytes=VMEM_LIMIT_BYTES,
                                flags=flags)


def _split_bf16(x):
    hi = x.astype(BF16)
    lo = (x - hi.astype(F32)).astype(BF16)
    return hi, lo


def _dot(a, b):
    return jnp.dot(a, b, preferred_element_type=F32)


def _dot2(x, w):
    hi, lo = _split_bf16(x)
    return _dot(hi, w) + _dot(lo, w)


def _rmsnorm_body(x_ref, g_ref, o_ref):
    x = x_ref[...]
    ms = jnp.mean(x * x, axis=-1, keepdims=True)
    o_ref[...] = (x * lax.rsqrt(ms + EPS) * g_ref[...]).astype(o_ref.dtype)


def _rmsnorm(x, g, tm):
    m, d = x.shape
    return pl.pallas_call(
        _rmsnorm_body,
        grid=(m // tm,),
        in_specs=[pl.BlockSpec((tm, d), lambda i: (i, 0)),
                  pl.BlockSpec((1, d), lambda i: (0, 0))],
        out_specs=pl.BlockSpec((tm, d), lambda i: (i, 0)),
        out_shape=jax.ShapeDtypeStruct((m, d), BF16),
        compiler_params=_params("parallel"),
        name="rmsnorm",
    )(x, g.reshape(1, d))


def _mm_body(*refs, n_pairs, qk_group, has_res):
    a_refs = refs[:n_pairs]
    w_refs = refs[n_pairs:2 * n_pairs]
    pos = 2 * n_pairs
    if qk_group:
        g_ref, ones_ref = refs[pos], refs[pos + 1]
        pos += 2
    if has_res:
        r_ref = refs[pos]
        pos += 1
    o_ref = refs[pos]

    acc = _dot(a_refs[0][...], w_refs[0][...])
    for p in range(1, n_pairs):
        acc = acc + _dot(a_refs[p][...], w_refs[p][...])
    if qk_group:
        sq = acc * acc
        hi, lo = _split_bf16(sq)
        ones = ones_ref[...]
        parts = []
        for s in range(acc.shape[1] // LANES):
            sl = slice(s * LANES, (s + 1) * LANES)
            parts.append(_dot(hi[:, sl], ones) + _dot(lo[:, sl], ones))
        gs = jnp.concatenate(parts, axis=1)
        acc = acc * lax.rsqrt(gs * (1.0 / qk_group) + EPS) * g_ref[...]
    if has_res:
        acc = acc + r_ref[...]
    o_ref[...] = acc.astype(o_ref.dtype)


def _matmul(a_list, w_list, *, n_out, tm, tn, out_dtype, w_col_off=0,
            gain=None, qk_group=0, res=None, name="matmul"):
    m = a_list[0].shape[0]
    tn = _tile(n_out, tn)
    off = w_col_off // tn
    in_specs, args = [], []
    for a in a_list:
        in_specs.append(pl.BlockSpec((tm, a.shape[1]), lambda i, j: (i, 0)))
        args.append(a)
    for w, rb, kr in w_list:
        in_specs.append(pl.BlockSpec((kr, tn), lambda i, j, rb=rb: (rb, j + off)))
        args.append(w)
    if qk_group:
        lane = jnp.arange(LANES)
        ones_bd = (lane[:, None] // qk_group == lane[None, :] // qk_group).astype(BF16)
        in_specs += [pl.BlockSpec((1, tn), lambda i, j: (0, 0)),
                     pl.BlockSpec((LANES, LANES), lambda i, j: (0, 0))]
        args += [jnp.tile(gain.astype(F32), tn // gain.shape[0]).reshape(1, tn), ones_bd]
    if res is not None:
        in_specs.append(pl.BlockSpec((tm, tn), lambda i, j: (i, j)))
        args.append(res)
    body = functools.partial(_mm_body, n_pairs=len(a_list), qk_group=qk_group,
                             has_res=res is not None)
    return pl.pallas_call(
        body,
        grid=(m // tm, n_out // tn),
        in_specs=in_specs,
        out_specs=pl.BlockSpec((tm, tn), lambda i, j: (i, j)),
        out_shape=jax.ShapeDtypeStruct((m, n_out), out_dtype),
        compiler_params=_params("parallel", "arbitrary"),
        name=name,
    )(*args)


_POOL_PAD = 16
_POOL_ROWS = 256


def _pool_prompt_body(u_ref, w_ref, sc_ref, o_ref, pad_ref, *, seq):
    g = pl.program_id(1)
    pad_ref[0:_POOL_PAD, :] = jnp.zeros((_POOL_PAD, pad_ref.shape[1]), F32)
    pad_ref[_POOL_PAD:, :] = u_ref[...]
    wmat = w_ref[0].astype(BF16)
    scale = sc_ref[...]
    for gi, win in enumerate(POOL_WINDOWS):
        @pl.when(g == gi)
        def _(win=win):
            for r0 in range(0, seq, _POOL_ROWS):
                cur = pad_ref[r0 + _POOL_PAD:r0 + _POOL_PAD + _POOL_ROWS, :]
                tot = cur
                for i in range(1, win):
                    tot = tot + pad_ref[r0 + _POOL_PAD - i:r0 + _POOL_PAD - i + _POOL_ROWS, :]
                pos = r0 + lax.broadcasted_iota(jnp.int32, (_POOL_ROWS, 1), 0)
                cnt = jnp.minimum(win, pos + 1).astype(F32)
                d = tot / cnt - cur
                y = _dot(d.astype(BF16), wmat) * scale
                o_ref[r0:r0 + _POOL_ROWS, :] = y.astype(o_ref.dtype)


def _pool_prompt(u, w_pool, pool_scale, batch, seq):
    n, width = u.shape
    ng = len(POOL_WINDOWS)
    gw = width // ng
    body = functools.partial(_pool_prompt_body, seq=seq)
    return pl.pallas_call(
        body,
        grid=(batch, ng),
        in_specs=[pl.BlockSpec((seq, gw), lambda b, g: (b, g)),
                  pl.BlockSpec((1, gw, gw), lambda b, g: (g, 0, 0)),
                  pl.BlockSpec((1, gw), lambda b, g: (0, g))],
        out_specs=pl.BlockSpec((seq, gw), lambda b, g: (b, g)),
        out_shape=jax.ShapeDtypeStruct((n, width), BF16),
        scratch_shapes=[pltpu.VMEM((seq + _POOL_PAD, gw), F32)],
        compiler_params=_params("parallel", "arbitrary"),
        name="pool_prompt",
    )(u, w_pool, pool_scale.reshape(1, width))


def _pool_sample_body(st_ref, u_ref, w_ref, sc_ref, o_ref, *, n_state, width):
    ng = len(POOL_WINDOWS)
    gw = width // ng
    for gi, win in enumerate(POOL_WINDOWS):
        c0 = gi * gw
        cur = u_ref[:, c0:c0 + gw]
        tot = cur
        for i in range(1, win):
            r = n_state - i
            tot = tot + st_ref[:, r * width + c0:r * width + c0 + gw]
        d = tot / float(win) - cur
        y = _dot(d.astype(BF16), w_ref[gi].astype(BF16)) * sc_ref[:, c0:c0 + gw]
        o_ref[:, c0:c0 + gw] = y.astype(o_ref.dtype)


def _pool_sample(state, u, w_pool, pool_scale):
    bd, n_state, width = state.shape
    body = functools.partial(_pool_sample_body, n_state=n_state, width=width)
    return pl.pallas_call(
        body,
        out_shape=jax.ShapeDtypeStruct((bd, width), BF16),
        compiler_params=pltpu.CompilerParams(vmem_limit_bytes=VMEM_LIMIT_BYTES),
        name="pool_sample",
    )(state.reshape(bd, n_state * width), u, w_pool, pool_scale.reshape(1, width))


def _rel_bucket(n):
    max_exact = NUM_BUCKETS // 2
    nf = jnp.maximum(n, 1).astype(F32)
    large = max_exact + (jnp.log(nf / max_exact) / math.log(MAX_DISTANCE / max_exact)
                         * (NUM_BUCKETS - max_exact)).astype(jnp.int32)
    large = jnp.minimum(large, NUM_BUCKETS - 1)
    return jnp.where(n < max_exact, n, large)


def _bias_by_distance(rel_bias, n_dist):
    return rel_bias.astype(F32)[_rel_bucket(jnp.arange(n_dist))]


def _lambda(lv_ref, lam_init):
    lv = lv_ref[...]
    a = jnp.sum(lv[0:1] * lv[1:2], axis=-1, keepdims=True)
    b = jnp.sum(lv[2:3] * lv[3:4], axis=-1, keepdims=True)
    return jnp.exp(a) - jnp.exp(b) + lam_init


def _attn_prompt_body(lv_ref, q_ref, k_ref, v_ref, bias_ref, g_ref, o_ref,
                      vt_sc, m_sc, l_sc, acc_sc, *, tq, tk, seq, sb, lam_init):
    qi = pl.program_id(2)

    @pl.when(qi == 0)
    def _transpose_values():
        for c in range(seq // tk):
            vt_sc[c] = v_ref[c * tk:(c + 1) * tk, :].T.astype(BF16)

    q = q_ref[...] * (DQK ** -0.5)
    lane = lax.broadcasted_iota(jnp.int32, q.shape, 1)
    q_ext = jnp.concatenate([jnp.where(lane < DQK, q, 0.0),
                             jnp.where(lane >= DQK, q, 0.0)], axis=0).astype(BF16)
    m_sc[...] = jnp.full(m_sc.shape, -jnp.inf, F32)
    l_sc[...] = jnp.zeros(l_sc.shape, F32)
    acc_sc[...] = jnp.zeros(acc_sc.shape, F32)

    def step(j, carry):
        r0 = pl.multiple_of(j * tk, tk)
        kj = k_ref[pl.ds(r0, tk), :].astype(BF16)
        st = lax.dot_general(kj, q_ext, (((1,), (1,)), ((), ())),
                             preferred_element_type=F32)
        d0 = (qi * tq - j * tk) // sb
        rows = []
        for a in range(tk // sb):
            row = [bias_ref[0, jnp.clip(d0 + b - a + 1, 0, 3)] for b in range(tq // sb)]
            rows.append(jnp.concatenate(row + row, axis=1))
        st = st + jnp.concatenate(rows, axis=0)
        m_old = m_sc[...]
        m_new = jnp.maximum(m_old, jnp.max(st, axis=0, keepdims=True))
        alpha = jnp.exp(m_old - m_new)
        p = jnp.exp(st - m_new)
        l_sc[...] = alpha * l_sc[...] + jnp.sum(p, axis=0, keepdims=True)
        acc_sc[...] = alpha * acc_sc[...] + _dot(vt_sc[j], p.astype(BF16))
        m_sc[...] = m_new
        return carry

    lax.fori_loop(0, (qi * tq) // tk + 1, step, 0)

    lam = _lambda(lv_ref, lam_init)
    o = acc_sc[...] / l_sc[...]
    att = (o[:, :tq] - lam * o[:, tq:]).T
    ms = jnp.mean(att * att, axis=-1, keepdims=True)
    out = att * lax.rsqrt(ms + EPS) * g_ref[...] * (1.0 - lam_init)
    o_ref[...] = out.astype(o_ref.dtype)


def _toeplitz(g, rows, cols):
    h = g.shape[0]
    period = rows + cols
    gp = jnp.pad(g, ((0, 0), (0, 1)))
    flat = jnp.tile(gp, (1, rows))[:, :rows * (period - 1)]
    return flat.reshape(h, rows, period - 1)[:, :, rows - 1:rows - 1 + cols]


def _attn_prompt(q, k, v, lam_vecs, rel_bias, subln_g, batch, seq, lam_init, tq=512, tk=512):
    n, width = q.shape
    heads = width // DV
    nq = seq // tq
    sb = MAX_DISTANCE
    assert seq % tk == 0 and tk % tq == 0 and tq % sb == 0 and sb == LANES
    bd = _bias_by_distance(rel_bias, 2 * sb).T
    line = jnp.concatenate([jnp.full((heads, sb - 1), -jnp.inf, F32), bd], axis=1)
    blocks = [jnp.full((heads, sb, sb), -jnp.inf, F32)]
    blocks += [_toeplitz(line[:, d * sb:d * sb + 2 * sb - 1], sb, sb) for d in range(2)]
    blocks.append(jnp.broadcast_to(bd[:, 2 * sb - 1][:, None, None], (heads, sb, sb)))
    bias_tiles = jnp.stack(blocks, axis=1)
    body = functools.partial(_attn_prompt_body, tq=tq, tk=tk, seq=seq, sb=sb,
                             lam_init=lam_init)
    return pl.pallas_call(
        body,
        grid=(batch, heads, nq),
        in_specs=[pl.BlockSpec((4, DQK), lambda b, h, i: (0, 0)),
                  pl.BlockSpec((tq, DV), lambda b, h, i: (b * nq + i, h)),
                  pl.BlockSpec((seq, DV), lambda b, h, i: (b, h)),
                  pl.BlockSpec((seq, DV), lambda b, h, i: (b, h)),
                  pl.BlockSpec((1, 4, sb, sb), lambda b, h, i: (h, 0, 0, 0)),
                  pl.BlockSpec((1, DV), lambda b, h, i: (0, 0))],
        out_specs=pl.BlockSpec((tq, DV), lambda b, h, i: (b * nq + i, h)),
        out_shape=jax.ShapeDtypeStruct((n, width), BF16),
        scratch_shapes=[pltpu.VMEM((seq // tk, DV, tk), BF16),
                        pltpu.VMEM((1, 2 * tq), F32), pltpu.VMEM((1, 2 * tq), F32),
                        pltpu.VMEM((DV, 2 * tq), F32)],
        compiler_params=_params("parallel", "parallel", "arbitrary"),
        name="attn_prompt",
    )(lam_vecs, q, k, v, bias_tiles, subln_g.reshape(1, DV))


_PAGE_SLOTS = 8
_PAGE_GROUP = 4


def _attn_decode_body(pt_ref, lv_ref, q_ref, kn_ref, vn_ref, bias_ref, bias0_ref, g_ref,
                      ck_hbm, cv_hbm, o_ref, ring, sem, p_sc, m_sc, l_sc, f_sc,
                      *, n_pages, n_seq, heads, page0, lam_init):
    b = pl.program_id(0)
    per_seq = 2 * n_pages

    def page_copy(seq, p):
        src = ck_hbm if p < n_pages else cv_hbm
        slot = p % _PAGE_SLOTS
        pid = page0 + pt_ref[seq, p % n_pages]
        return pltpu.make_async_copy(src.at[pid], ring.at[slot], sem.at[slot])

    @pl.when(b == 0)
    def _prime():
        for p in range(_PAGE_SLOTS):
            page_copy(0, p).start()

    def consume(p0, fn):
        group = range(p0, p0 + _PAGE_GROUP)
        for p in group:
            page_copy(b, p).wait()
        outs = [fn(ring[p % _PAGE_SLOTS], p) for p in group]
        for p in group:
            nxt = p + _PAGE_SLOTS
            if nxt < per_seq:
                page_copy(b, nxt).start()
            else:
                @pl.when(b + 1 < n_seq)
                def _next_seq(nxt=nxt):
                    page_copy(b + 1, nxt - per_seq).start()
        return outs

    q = q_ref[0] * (DQK ** -0.5)
    lane = lax.broadcasted_iota(jnp.int32, q.shape, 1)
    q_all = jnp.concatenate([jnp.where(lane < DQK, q, 0.0),
                             jnp.where(lane >= DQK, q, 0.0)], axis=0)
    qb = q_all.astype(BF16)
    stat = m_sc.shape[1:]

    def scores(page, pg):
        sc = lax.dot_general(qb, page.astype(BF16), (((1,), (1,)), ((), ())),
                             preferred_element_type=F32)
        sc = sc + bias_ref[pg]
        m_pg = jnp.max(sc, axis=1, keepdims=True)
        p = jnp.exp(sc - m_pg)
        p_sc[pg] = p
        m_sc[pg] = jnp.broadcast_to(m_pg, stat)
        l_sc[pg] = jnp.broadcast_to(jnp.sum(p, axis=1, keepdims=True), stat)

    for p0 in range(0, n_pages, _PAGE_GROUP):
        consume(p0, scores)

    kn = kn_ref[0]
    sn = (jnp.sum(q_all * jnp.concatenate([kn, kn], axis=0), axis=1, keepdims=True)
          + bias0_ref[...])
    m_all = m_sc[...]
    m = jnp.maximum(jnp.max(m_all, axis=0), sn)
    e = jnp.exp(m_all - m[None])
    pn = jnp.exp(sn - m)
    l = jnp.sum(l_sc[...] * e, axis=0) + pn
    lam = _lambda(lv_ref, lam_init)
    r = jnp.concatenate([1.0 / l[:heads], lam / l[heads:]], axis=0)
    f_sc[...] = e * r[None]
    wn = pn * r

    reps = p_sc.shape[2] // LANES
    acc = (wn[:heads] - wn[heads:]) * vn_ref[0]
    def values(page, p):
        pg = p - n_pages
        pw = p_sc[pg] * jnp.tile(f_sc[pg], (1, reps))
        w = (pw[:heads] - pw[heads:]).astype(BF16)
        return _dot(w, page.astype(BF16))

    for p0 in range(n_pages, per_seq, _PAGE_GROUP):
        for part in consume(p0, values):
            acc = acc + part

    ms = jnp.mean(acc * acc, axis=-1, keepdims=True)
    o_ref[0] = acc * lax.rsqrt(ms + EPS) * g_ref[...] * (1.0 - lam_init)


def _attn_decode(q, k_new, v_new, cache_k, cache_v, layer, page_table, lam_vecs, rel_bias,
                 subln_g, lam_init):
    bd, width = q.shape
    heads = width // DV
    n_pages = page_table.shape[1]
    past = n_pages * PAGE_SIZE
    assert (2 * n_pages) % _PAGE_SLOTS == 0, "ring slot of a stream page must not depend on b"
    depth, n_pool = cache_k.shape[:2]
    ck = cache_k.reshape(depth * n_pool, PAGE_SIZE * heads, DV)
    cv = cache_v.reshape(depth * n_pool, PAGE_SIZE * heads, DV)
    page0 = layer * n_pool
    bdist = _bias_by_distance(rel_bias, past + 1)
    b_past = bdist[1:][::-1].reshape(n_pages, 1, PAGE_SIZE, heads)
    own_head = jnp.eye(heads, dtype=bool)[None, :, None, :]
    b_past = jnp.where(own_head, b_past, -jnp.inf).reshape(n_pages, heads, PAGE_SIZE * heads)
    b_past = jnp.concatenate([b_past, b_past], axis=1)
    b_new = jnp.broadcast_to(jnp.tile(bdist[0], 2)[:, None], (2 * heads, LANES))

    row3 = lambda b, pt: (b, 0, 0)
    const2 = lambda b, pt: (0, 0)
    page_rows = PAGE_SIZE * heads
    in_specs = [pl.BlockSpec((4, DQK), const2),
                pl.BlockSpec((1, heads, DV), row3),
                pl.BlockSpec((1, heads, DV), row3),
                pl.BlockSpec((1, heads, DV), row3),
                pl.BlockSpec((n_pages, 2 * heads, page_rows), lambda b, pt: (0, 0, 0),
                             pipeline_mode=pl.Buffered(1)),
                pl.BlockSpec((2 * heads, LANES), const2),
                pl.BlockSpec((1, DV), const2),
                pl.BlockSpec(memory_space=pl.ANY),
                pl.BlockSpec(memory_space=pl.ANY)]
    body = functools.partial(_attn_decode_body, n_pages=n_pages, n_seq=bd, heads=heads,
                             page0=page0, lam_init=lam_init)
    out = pl.pallas_call(
        body,
        grid_spec=pltpu.PrefetchScalarGridSpec(
            num_scalar_prefetch=1,
            grid=(bd,),
            in_specs=in_specs,
            out_specs=pl.BlockSpec((1, heads, DV), row3),
            scratch_shapes=[pltpu.VMEM((_PAGE_SLOTS, page_rows, DV), F32),
                            pltpu.SemaphoreType.DMA((_PAGE_SLOTS,)),
                            pltpu.VMEM((n_pages, 2 * heads, page_rows), F32),
                            pltpu.VMEM((n_pages, 2 * heads, LANES), F32),
                            pltpu.VMEM((n_pages, 2 * heads, LANES), F32),
                            pltpu.VMEM((n_pages, 2 * heads, LANES), F32)]),
        out_shape=jax.ShapeDtypeStruct((bd, heads, DV), F32),
        compiler_params=_params("arbitrary"),
        name="attn_decode",
    )(page_table, lam_vecs, q.reshape(bd, heads, DV), k_new.reshape(bd, heads, DV),
      v_new.reshape(bd, heads, DV), b_past, b_new, subln_g.reshape(1, DV), ck, cv)
    return out.reshape(bd, width)


def _topk_rows(s, k, ties):
    rows = s.shape[0]
    rank = jnp.full(s.shape, float(k), F32)
    if ties:
        iota = lax.broadcasted_iota(jnp.int32, s.shape, 0).astype(F32)
    vals = []
    for p in range(k):
        m = jnp.max(s, axis=0, keepdims=True)
        sel = s == m
        if ties:
            idx = jnp.min(jnp.where(sel, iota, float(rows)), axis=0, keepdims=True)
            sel = iota == idx
        rank = jnp.where(sel, float(p), rank)
        s = jnp.where(sel, -jnp.inf, s)
        vals.append(m)
    return vals, rank


def _route_tables(s1, s2, ties):
    k = PEER_TOPK
    v1, rank1 = _topk_rows(s1, k, ties)
    v2, rank2 = _topk_rows(s2, k, ties)
    v1_all = jnp.concatenate(v1, axis=0)
    v2_all = jnp.concatenate(v2, axis=0)
    sub = lax.broadcasted_iota(jnp.int32, (8, s1.shape[1]), 0)
    blocks, spans, row = [], [], 0
    p = 0
    while p < k:
        nq = k // (p + 1)
        if nq >= 8:
            nb = -(-nq // 8) * 8
            blk = v1[p] + v2_all[0:nb]
            if nb != nq:
                blk = jnp.where(jnp.concatenate([sub + 8 * i for i in range(nb // 8)]) < nq,
                                blk, -jnp.inf)
            blocks.append(blk)
            spans.append((p, row, nb))
            row += nb
            p += 1
        elif nq > 1:
            blocks.append(jnp.where(sub < nq, v1[p] + v2_all[0:8], -jnp.inf))
            spans.append((p, row, 8))
            row += 8
            p += 1
        else:
            assert (k - p) == 8, "single-candidate rows are packed as one aligned 8-row block"
            blocks.append(v1_all[p:p + 8] + v2[0])
            for r in range(8):
                spans.append((p + r, row + r, 1))
            row += 8
            p += 8
    cand = jnp.concatenate(blocks, axis=0)
    top, rank_c = _topk_rows(cand, k, ties)
    chosen = jnp.where(rank_c < float(k), 1.0, 0.0)
    z = jnp.zeros_like(top[0])
    for i in range(k):
        z = z + jnp.exp(top[i] - top[0])
    n1 = jnp.zeros(s1.shape, F32)
    for pp, r0, nr in spans:
        n_p = jnp.sum(chosen[r0:r0 + nr], axis=0, keepdims=True)
        n1 = jnp.where(rank1 == float(pp), n_p, n1)
    marked = (jnp.sum(jnp.where(rank1 < float(k), 1.0, 0.0), axis=0, keepdims=True)
              + jnp.sum(jnp.where(rank2 < float(k), 1.0, 0.0), axis=0, keepdims=True)
              + jnp.sum(chosen, axis=0, keepdims=True))
    tables = (n1, jnp.exp(s1 - v1[0]) / z, rank2, jnp.exp(s2 - v2[0]))
    return tables, marked - 3.0 * k


def _route_body(qr_ref, keys_ref, n1_ref, w1_ref, r2_ref, w2_ref):
    nt = (((1,), (1,)), ((), ()))
    s1 = lax.dot_general(keys_ref[0], qr_ref[:, 0:LANES], nt, preferred_element_type=F32)
    s2 = lax.dot_general(keys_ref[1], qr_ref[:, LANES:2 * LANES], nt, preferred_element_type=F32)
    out_refs = (n1_ref, w1_ref, r2_ref, w2_ref)
    tables, extra = _route_tables(s1, s2, ties=False)
    tied = jnp.max(extra) > 0.0
    for ref, t in zip(out_refs, tables):
        ref[0] = t

    @pl.when(tied)
    def _exact():
        exact_tables, _ = _route_tables(s1, s2, ties=True)
        for ref, t in zip(out_refs, exact_tables):
            ref[0] = t


def _route(qr, sub_keys, tt):
    n = qr.shape[0]
    heads, _, n_keys, d_half = sub_keys.shape
    keys = sub_keys.reshape(heads * 2, n_keys, d_half).astype(BF16)
    out = jax.ShapeDtypeStruct((heads, n_keys, n), F32)
    ospec = pl.BlockSpec((1, n_keys, tt), lambda i, h: (h, 0, i))
    return pl.pallas_call(
        _route_body,
        grid=(n // tt, heads),
        in_specs=[pl.BlockSpec((tt, 2 * d_half), lambda i, h: (i, h)),
                  pl.BlockSpec((2, n_keys, d_half), lambda i, h: (h, 0, 0))],
        out_specs=[ospec, ospec, ospec, ospec],
        out_shape=[out, out, out, out],
        compiler_params=_params("parallel", "arbitrary"),
        name="peer_route",
    )(qr, keys)


def _gelu(x):
    return 0.5 * x * (1.0 + lax.erf(x * (2.0 ** -0.5)))


def _peer_body(h_ref, u_ref, v_ref, n1_ref, w1_ref, r2_ref, w2_ref, x_ref, o_ref,
               z0_sc, z1_sc, a_sc, *, ec, tt, heads, n_keys):
    j = pl.program_id(1)
    nsl = ec // n_keys

    @pl.when(j == 0)
    def _init():
        o_ref[...] = x_ref[...]
        z1_sc[...] = jnp.zeros(z1_sc.shape, F32)

    def stage(z_next, z_prev):
        z_next[...] = lax.dot_general(u_ref[...], h_ref[...], (((1,), (1,)), ((), ())),
                                      preferred_element_type=F32)
        for sl in range(nsl):
            for tb in range(tt // LANES):
                cols = slice(tb * LANES, (tb + 1) * LANES)
                g = jnp.zeros((n_keys, LANES), F32)
                for h in range(heads):
                    row = h * nsl + sl
                    n1 = n1_ref[0, row:row + 1, cols]
                    w1 = w1_ref[0, row:row + 1, cols]
                    g = g + jnp.where(r2_ref[h, :, cols] < n1, w2_ref[h, :, cols] * w1, 0.0)
                z = z_prev[sl * n_keys:(sl + 1) * n_keys, cols]
                a_sc[sl * n_keys:(sl + 1) * n_keys, cols] = (_gelu(z) * g).astype(a_sc.dtype)
        o_ref[...] += lax.dot_general(a_sc[...], v_ref[...], (((0,), (0,)), ((), ())),
                                      preferred_element_type=F32)

    @pl.when(j % 2 == 0)
    def _even():
        stage(z0_sc, z1_sc)

    @pl.when(j % 2 == 1)
    def _odd():
        stage(z1_sc, z0_sc)


def _peer(h2, u_bf, v_bf, tables, x, tt, ec):
    n, d = h2.shape
    n_exp = u_bf.shape[0]
    n1, w1, r2, w2 = tables
    heads, n_keys, _ = n1.shape
    nsl = ec // n_keys
    nc = n_exp // ec

    def by_chunk(t):
        t = t.reshape(heads, n_keys // nsl, nsl, n)
        return jnp.transpose(t, (1, 0, 2, 3)).reshape(n_keys // nsl, heads * nsl, n)

    once = pl.Buffered(1)
    tspec = pl.BlockSpec((heads, n_keys, tt), lambda i, j: (0, 0, i), pipeline_mode=once)
    prev = lambda j: jnp.maximum(j - 1, 0)
    cspec = pl.BlockSpec((1, heads * nsl, tt), lambda i, j: (prev(j), 0, i))
    body = functools.partial(_peer_body, ec=ec, tt=tt, heads=heads, n_keys=n_keys)
    return pl.pallas_call(
        body,
        grid=(n // tt, nc + 1),
        in_specs=[pl.BlockSpec((tt, d), lambda i, j: (i, 0), pipeline_mode=once),
                  pl.BlockSpec((ec, d), lambda i, j: (jnp.minimum(j, nc - 1), 0)),
                  pl.BlockSpec((ec, d), lambda i, j: (prev(j), 0)),
                  cspec, cspec, tspec, tspec,
                  pl.BlockSpec((tt, d), lambda i, j: (i, 0), pipeline_mode=once)],
        out_specs=pl.BlockSpec((tt, d), lambda i, j: (i, 0)),
        out_shape=jax.ShapeDtypeStruct((n, d), F32),
        scratch_shapes=[pltpu.VMEM((ec, tt), F32), pltpu.VMEM((ec, tt), F32),
                        pltpu.VMEM((ec, tt), BF16)],
        compiler_params=_params("parallel", "arbitrary"),
        name="peer_experts",
    )(h2, u_bf, v_bf, by_chunk(n1), by_chunk(w1), r2, w2, x)


def _tile(n, pref):
    return pref if n % pref == 0 else n


def _project(x2d, ln_g, w_in_bf, qn_g, kn_g, widths):
    pool_w, attn_w = widths
    n, d = x2d.shape
    tm = _tile(n, 1024)
    xn = _rmsnorm(x2d, ln_g, _tile(n, 256))
    mm = functools.partial(_matmul, [xn], [(w_in_bf, 0, d)], tm=tm, tn=512, out_dtype=F32)
    u = mm(n_out=pool_w, w_col_off=0, name="proj_u")
    q = mm(n_out=attn_w, w_col_off=pool_w, gain=qn_g, qk_group=DQK, name="proj_q")
    k = mm(n_out=attn_w, w_col_off=pool_w + attn_w, gain=kn_g, qk_group=DQK, name="proj_k")
    v = mm(n_out=attn_w, w_col_off=pool_w + 2 * attn_w, name="proj_v")
    return u, q, k, v


def _mix_and_peer(x2d, pool_y, att, w_o_bf, ln2_g, w_query_bf, sub_keys, u_bf, v_bf):
    n, d = x2d.shape
    pool_w = pool_y.shape[1]
    tm = _tile(n, 1024)
    assert att.shape[1] == pool_w, "w_o row blocks assume equal pool and attention widths"
    x1 = _matmul([pool_y, att], [(w_o_bf, 0, pool_w), (w_o_bf, 1, pool_w)],
                 n_out=d, tm=tm, tn=512, out_dtype=F32, res=x2d, name="out_proj")
    h2 = _rmsnorm(x1, ln2_g, _tile(n, 256))
    qr = _matmul([h2], [(w_query_bf, 0, d)], n_out=w_query_bf.shape[1], tm=tm, tn=512,
                 out_dtype=BF16, name="peer_query")
    tables = _route(qr, sub_keys, _tile(n, LANES))
    return _peer(h2, u_bf, v_bf, tables, x1, _tile(n, 512), 512)


def kernel(x_prompt, x_sample, cache_k, cache_v, state_pool, page_table, rel_bias, ln1_g, w_in, q_norm_g, k_norm_g, lambda_q1, lambda_k1, lambda_q2, lambda_k2, subln_g, w_pool, pool_scale, w_o, ln2_g, w_query, sub_keys, peer_u, peer_v):
    batch, seq, d = x_prompt.shape
    bd, tq, _ = x_sample.shape
    assert tq == 1, "sample group is one new position per sequence"
    depth = w_in.shape[0]
    pool_w = w_pool.shape[1] * w_pool.shape[2]
    attn_w = (w_in.shape[2] - pool_w) // 3
    heads = attn_w // DV
    n_state = state_pool.shape[2]

    xp = x_prompt.reshape(batch * seq, d)
    xs = x_sample.reshape(bd, d)
    kp_l, vp_l, pp_l, ks_l, vs_l, ps_l = [], [], [], [], [], []
    for l in range(depth):
        lam_init = 0.8 - 0.6 * math.exp(-0.3 * l)
        lam_vecs = jnp.stack([lambda_q1[l], lambda_k1[l], lambda_q2[l], lambda_k2[l]]).astype(F32)
        w_in_bf = w_in[l].astype(BF16)
        w_o_bf = w_o[l].astype(BF16)
        w_query_bf = w_query[l].astype(BF16)
        u_bf = peer_u[l].astype(BF16)
        v_bf = peer_v[l].astype(BF16)

        u, q, k, v = _project(xp, ln1_g[l], w_in_bf, q_norm_g[l], k_norm_g[l], (pool_w, attn_w))
        pool_y = _pool_prompt(u, w_pool[l], pool_scale[l], batch, seq)
        att = _attn_prompt(q, k, v, lam_vecs, rel_bias, subln_g[l], batch, seq, lam_init)
        xp = _mix_and_peer(xp, pool_y, att, w_o_bf, ln2_g[l], w_query_bf, sub_keys[l], u_bf, v_bf)
        kp_l.append(k.reshape(batch, seq, heads, 2 * DQK))
        vp_l.append(v.reshape(batch, seq, heads, DV))
        pp_l.append(u.reshape(batch, seq, pool_w)[:, seq - n_state:])

        u, q, k, v = _project(xs, ln1_g[l], w_in_bf, q_norm_g[l], k_norm_g[l], (pool_w, attn_w))
        pool_y = _pool_sample(state_pool[l], u, w_pool[l], pool_scale[l])
        att = _attn_decode(q, k, v, cache_k, cache_v, l, page_table, lam_vecs, rel_bias,
                           subln_g[l], lam_init).astype(BF16)
        xs = _mix_and_peer(xs, pool_y, att, w_o_bf, ln2_g[l], w_query_bf, sub_keys[l], u_bf, v_bf)
        ks_l.append(k.reshape(bd, tq, heads, 2 * DQK))
        vs_l.append(v.reshape(bd, tq, heads, DV))
        ps_l.append(jnp.concatenate([state_pool[l], u[:, None, :]], axis=1)[:, -n_state:])

    return (xp.reshape(batch, seq, d), xs.reshape(bd, tq, d),
            jnp.stack(kp_l), jnp.stack(vp_l), jnp.stack(pp_l),
            jnp.stack(ks_l), jnp.stack(vs_l), jnp.stack(ps_l))
```

```python
import functools
import math

import jax
import jax.numpy as jnp
from jax import lax
from jax.experimental import pallas as pl
from jax.experimental.pallas import tpu as pltpu

F32 = jnp.float32
BF16 = jnp.bfloat16
EPS = 1e-6

LANES = 128
VMEM_LIMIT_BYTES = 56 * 1024 * 1024

POOL_WINDOWS = (2, 4, 8, 16)
DQK = 64
DV = 128
NUM_BUCKETS = 32
MAX_DISTANCE = 128
PEER_TOPK = 16
PAGE_SIZE = 128


def _params(*sem, flags=None):
    return pltpu.CompilerParams(dimension_semantics=sem, vmem_limit_bytes=VMEM_LIMIT_BYTES,
                                flags=flags)


def _split_bf16(x):
    hi = x.astype(BF16)
    lo = (x - hi.astype(F32)).astype(BF16)
    return hi, lo


def _dot(a, b):
    return jnp.dot(a, b, preferred_element_type=F32)


def _dot2(x, w):
    hi, lo = _split_bf16(x)
    return _dot(hi, w) + _dot(lo, w)


def _rmsnorm_body(x_ref, g_ref, o_ref):
    x = x_ref[...]
    ms = jnp.mean(x * x, axis=-1, keepdims=True)
    o_ref[...] = (x * lax.rsqrt(ms + EPS) * g_ref[...]).astype(o_ref.dtype)


def _rmsnorm(x, g, tm):
    m, d = x.shape
    return pl.pallas_call(
        _rmsnorm_body,
        grid=(m // tm,),
        in_specs=[pl.BlockSpec((tm, d), lambda i: (i, 0)),
                  pl.BlockSpec((1, d), lambda i: (0, 0))],
        out_specs=pl.BlockSpec((tm, d), lambda i: (i, 0)),
        out_shape=jax.ShapeDtypeStruct((m, d), BF16),
        compiler_params=_params("parallel"),
        name="rmsnorm",
    )(x, g.reshape(1, d))


def _mm_body(*refs, n_pairs, qk_group, has_res):
    a_refs = refs[:n_pairs]
    w_refs = refs[n_pairs:2 * n_pairs]
    pos = 2 * n_pairs
    if qk_group:
        g_ref, ones_ref = refs[pos], refs[pos + 1]
        pos += 2
    if has_res:
        r_ref = refs[pos]
        pos += 1
    o_ref = refs[pos]

    acc = _dot(a_refs[0][...], w_refs[0][...])
    for p in range(1, n_pairs):
        acc = acc + _dot(a_refs[p][...], w_refs[p][...])
    if qk_group:
        sq = acc * acc
        hi, lo = _split_bf16(sq)
        ones = ones_ref[...]
        parts = []
        for s in range(acc.shape[1] // LANES):
            sl = slice(s * LANES, (s + 1) * LANES)
            parts.append(_dot(hi[:, sl], ones) + _dot(lo[:, sl], ones))
        gs = jnp.concatenate(parts, axis=1)
        acc = acc * lax.rsqrt(gs * (1.0 / qk_group) + EPS) * g_ref[...]
    if has_res:
        acc = acc + r_ref[...]
    o_ref[...] = acc.astype(o_ref.dtype)


def _matmul(a_list, w_list, *, n_out, tm, tn, out_dtype, w_col_off=0,
            gain=None, qk_group=0, res=None, name="matmul"):
    m = a_list[0].shape[0]
    tn = _tile(n_out, tn)
    off = w_col_off // tn
    in_specs, args = [], []
    for a in a_list:
        in_specs.append(pl.BlockSpec((tm, a.shape[1]), lambda i, j: (i, 0)))
        args.append(a)
    for w, rb, kr in w_list:
        in_specs.append(pl.BlockSpec((kr, tn), lambda i, j, rb=rb: (rb, j + off)))
        args.append(w)
    if qk_group:
        lane = jnp.arange(LANES)
        ones_bd = (lane[:, None] // qk_group == lane[None, :] // qk_group).astype(BF16)
        in_specs += [pl.BlockSpec((1, tn), lambda i, j: (0, 0)),
                     pl.BlockSpec((LANES, LANES), lambda i, j: (0, 0))]
        args += [jnp.tile(gain.astype(F32), tn // gain.shape[0]).reshape(1, tn), ones_bd]
    if res is not None:
        in_specs.append(pl.BlockSpec((tm, tn), lambda i, j: (i, j)))
        args.append(res)
    body = functools.partial(_mm_body, n_pairs=len(a_list), qk_group=qk_group,
                             has_res=res is not None)
    return pl.pallas_call(
        body,
        grid=(m // tm, n_out // tn),
        in_specs=in_specs,
        out_specs=pl.BlockSpec((tm, tn), lambda i, j: (i, j)),
        out_shape=jax.ShapeDtypeStruct((m, n_out), out_dtype),
        compiler_params=_params("parallel", "arbitrary"),
        name=name,
    )(*args)


_POOL_PAD = 16
_POOL_ROWS = 256


def _pool_prompt_body(u_ref, w_ref, sc_ref, o_ref, pad_ref, *, seq):
    g = pl.program_id(1)
    pad_ref[0:_POOL_PAD, :] = jnp.zeros((_POOL_PAD, pad_ref.shape[1]), F32)
    pad_ref[_POOL_PAD:, :] = u_ref[...]
    wmat = w_ref[0].astype(BF16)
    scale = sc_ref[...]
    for gi, win in enumerate(POOL_WINDOWS):
        @pl.when(g == gi)
        def _(win=win):
            for r0 in range(0, seq, _POOL_ROWS):
                cur = pad_ref[r0 + _POOL_PAD:r0 + _POOL_PAD + _POOL_ROWS, :]
                tot = cur
                for i in range(1, win):
                    tot = tot + pad_ref[r0 + _POOL_PAD - i:r0 + _POOL_PAD - i + _POOL_ROWS, :]
                pos = r0 + lax.broadcasted_iota(jnp.int32, (_POOL_ROWS, 1), 0)
                cnt = jnp.minimum(win, pos + 1).astype(F32)
                d = tot / cnt - cur
                y = _dot(d.astype(BF16), wmat) * scale
                o_ref[r0:r0 + _POOL_ROWS, :] = y.astype(o_ref.dtype)


def _pool_prompt(u, w_pool, pool_scale, batch, seq):
    n, width = u.shape
    ng = len(POOL_WINDOWS)
    gw = width // ng
    body = functools.partial(_pool_prompt_body, seq=seq)
    return pl.pallas_call(
        body,
        grid=(batch, ng),
        in_specs=[pl.BlockSpec((seq, gw), lambda b, g: (b, g)),
                  pl.BlockSpec((1, gw, gw), lambda b, g: (g, 0, 0)),
                  pl.BlockSpec((1, gw), lambda b, g: (0, g))],
        out_specs=pl.BlockSpec((seq, gw), lambda b, g: (b, g)),
        out_shape=jax.ShapeDtypeStruct((n, width), BF16),
        scratch_shapes=[pltpu.VMEM((seq + _POOL_PAD, gw), F32)],
        compiler_params=_params("parallel", "arbitrary"),
        name="pool_prompt",
    )(u, w_pool, pool_scale.reshape(1, width))


def _pool_sample_body(st_ref, u_ref, w_ref, sc_ref, o_ref, *, n_state, width):
    ng = len(POOL_WINDOWS)
    gw = width // ng
    for gi, win in enumerate(POOL_WINDOWS):
        c0 = gi * gw
        cur = u_ref[:, c0:c0 + gw]
        tot = cur
        for i in range(1, win):
            r = n_state - i
            tot = tot + st_ref[:, r * width + c0:r * width + c0 + gw]
        d = tot / float(win) - cur
        y = _dot(d.astype(BF16), w_ref[gi].astype(BF16)) * sc_ref[:, c0:c0 + gw]
        o_ref[:, c0:c0 + gw] = y.astype(o_ref.dtype)


def _pool_sample(state, u, w_pool, pool_scale):
    bd, n_state, width = state.shape
    body = functools.partial(_pool_sample_body, n_state=n_state, width=width)
    return pl.pallas_call(
        body,
        out_shape=jax.ShapeDtypeStruct((bd, width), BF16),
        compiler_params=pltpu.CompilerParams(vmem_limit_bytes=VMEM_LIMIT_BYTES),
        name="pool_sample",
    )(state.reshape(bd, n_state * width), u, w_pool, pool_scale.reshape(1, width))


def _rel_bucket(n):
    max_exact = NUM_BUCKETS // 2
    nf = jnp.maximum(n, 1).astype(F32)
    large = max_exact + (jnp.log(nf / max_exact) / math.log(MAX_DISTANCE / max_exact)
                         * (NUM_BUCKETS - max_exact)).astype(jnp.int32)
    large = jnp.minimum(large, NUM_BUCKETS - 1)
    return jnp.where(n < max_exact, n, large)


def _bias_by_distance(rel_bias, n_dist):
    return rel_bias.astype(F32)[_rel_bucket(jnp.arange(n_dist))]


def _lambda(lv_ref, lam_init):
    lv = lv_ref[...]
    a = jnp.sum(lv[0:1] * lv[1:2], axis=-1, keepdims=True)
    b = jnp.sum(lv[2:3] * lv[3:4], axis=-1, keepdims=True)
    return jnp.exp(a) - jnp.exp(b) + lam_init


def _attn_prompt_body(lv_ref, q_ref, k_ref, v_ref, bias_ref, g_ref, o_ref,
                      vt_sc, m_sc, l_sc, acc_sc, *, tq, tk, seq, sb, lam_init):
    qi = pl.program_id(2)

    @pl.when(qi == 0)
    def _transpose_values():
        for c in range(seq // tk):
            vt_sc[c] = v_ref[c * tk:(c + 1) * tk, :].T.astype(BF16)

    q = q_ref[...] * (DQK ** -0.5)
    lane = lax.broadcasted_iota(jnp.int32, q.shape, 1)
    q_ext = jnp.concatenate([jnp.where(lane < DQK, q, 0.0),
                             jnp.where(lane >= DQK, q, 0.0)], axis=0).astype(BF16)
    m_sc[...] = jnp.full(m_sc.shape, -jnp.inf, F32)
    l_sc[...] = jnp.zeros(l_sc.shape, F32)
    acc_sc[...] = jnp.zeros(acc_sc.shape, F32)

    def step(j, carry):
        r0 = pl.multiple_of(j * tk, tk)
        kj = k_ref[pl.ds(r0, tk), :].astype(BF16)
        st = lax.dot_general(kj, q_ext, (((1,), (1,)), ((), ())),
                             preferred_element_type=F32)
        d0 = (qi * tq - j * tk) // sb
        rows = []
        for a in range(tk // sb):
            row = [bias_ref[0, jnp.clip(d0 + b - a + 1, 0, 3)] for b in range(tq // sb)]
            rows.append(jnp.concatenate(row + row, axis=1))
        st = st + jnp.concatenate(rows, axis=0)
        m_old = m_sc[...]
        m_new = jnp.maximum(m_old, jnp.max(st, axis=0, keepdims=True))
        alpha = jnp.exp(m_old - m_new)
        p = jnp.exp(st - m_new)
        l_sc[...] = alpha * l_sc[...] + jnp.sum(p, axis=0, keepdims=True)
        acc_sc[...] = alpha * acc_sc[...] + _dot(vt_sc[j], p.astype(BF16))
        m_sc[...] = m_new
        return carry

    lax.fori_loop(0, (qi * tq) // tk + 1, step, 0)

    lam = _lambda(lv_ref, lam_init)
    o = acc_sc[...] / l_sc[...]
    att = (o[:, :tq] - lam * o[:, tq:]).T
    ms = jnp.mean(att * att, axis=-1, keepdims=True)
    out = att * lax.rsqrt(ms + EPS) * g_ref[...] * (1.0 - lam_init)
    o_ref[...] = out.astype(o_ref.dtype)


def _toeplitz(g, rows, cols):
    h = g.shape[0]
    period = rows + cols
    gp = jnp.pad(g, ((0, 0), (0, 1)))
    flat = jnp.tile(gp, (1, rows))[:, :rows * (period - 1)]
    return flat.reshape(h, rows, period - 1)[:, :, rows - 1:rows - 1 + cols]


def _attn_prompt(q, k, v, lam_vecs, rel_bias, subln_g, batch, seq, lam_init, tq=512, tk=512):
    n, width = q.shape
    heads = width // DV
    nq = seq // tq
    sb = MAX_DISTANCE
    assert seq % tk == 0 and tk % tq == 0 and tq % sb == 0 and sb == LANES
    bd = _bias_by_distance(rel_bias, 2 * sb).T
    line = jnp.concatenate([jnp.full((heads, sb - 1), -jnp.inf, F32), bd], axis=1)
    blocks = [jnp.full((heads, sb, sb), -jnp.inf, F32)]
    blocks += [_toeplitz(line[:, d * sb:d * sb + 2 * sb - 1], sb, sb) for d in range(2)]
    blocks.append(jnp.broadcast_to(bd[:, 2 * sb - 1][:, None, None], (heads, sb, sb)))
    bias_tiles = jnp.stack(blocks, axis=1)
    body = functools.partial(_attn_prompt_body, tq=tq, tk=tk, seq=seq, sb=sb,
                             lam_init=lam_init)
    return pl.pallas_call(
        body,
        grid=(batch, heads, nq),
        in_specs=[pl.BlockSpec((4, DQK), lambda b, h, i: (0, 0)),
                  pl.BlockSpec((tq, DV), lambda b, h, i: (b * nq + i, h)),
                  pl.BlockSpec((seq, DV), lambda b, h, i: (b, h)),
                  pl.BlockSpec((seq, DV), lambda b, h, i: (b, h)),
                  pl.BlockSpec((1, 4, sb, sb), lambda b, h, i: (h, 0, 0, 0)),
                  pl.BlockSpec((1, DV), lambda b, h, i: (0, 0))],
        out_specs=pl.BlockSpec((tq, DV), lambda b, h, i: (b * nq + i, h)),
        out_shape=jax.ShapeDtypeStruct((n, width), BF16),
        scratch_shapes=[pltpu.VMEM((seq // tk, DV, tk), BF16),
                        pltpu.VMEM((1, 2 * tq), F32), pltpu.VMEM((1, 2 * tq), F32),
                        pltpu.VMEM((DV, 2 * tq), F32)],
        compiler_params=_params("parallel", "parallel", "arbitrary"),
        name="attn_prompt",
    )(lam_vecs, q, k, v, bias_tiles, subln_g.reshape(1, DV))


_PAGE_SLOTS = 16
_PAGE_GROUP = 4


def _attn_decode_body(pt_ref, lv_ref, q_ref, kn_ref, vn_ref, bias_ref, bias0_ref, g_ref,
                      ck_hbm, cv_hbm, o_ref, ring, sem, p_sc, m_sc, l_sc, f_sc,
                      *, n_pages, n_seq, heads, page0, lam_init):
    b = pl.program_id(0)
    per_seq = 2 * n_pages

    def page_copy(seq, p):
        src = ck_hbm if p < n_pages else cv_hbm
        slot = p % _PAGE_SLOTS
        pid = page0 + pt_ref[seq, p % n_pages]
        return pltpu.make_async_copy(src.at[pid], ring.at[slot], sem.at[slot])

    @pl.when(b == 0)
    def _prime():
        for p in range(_PAGE_SLOTS):
            page_copy(0, p).start()

    def consume(p0, fn):
        group = range(p0, p0 + _PAGE_GROUP)
        for p in group:
            page_copy(b, p).wait()
        outs = [fn(ring[p % _PAGE_SLOTS], p) for p in group]
        for p in group:
            nxt = p + _PAGE_SLOTS
            if nxt < per_seq:
                page_copy(b, nxt).start()
            else:
                @pl.when(b + 1 < n_seq)
                def _next_seq(nxt=nxt):
                    page_copy(b + 1, nxt - per_seq).start()
        return outs

    q = q_ref[0] * (DQK ** -0.5)
    lane = lax.broadcasted_iota(jnp.int32, q.shape, 1)
    q_all = jnp.concatenate([jnp.where(lane < DQK, q, 0.0),
                             jnp.where(lane >= DQK, q, 0.0)], axis=0)
    qb = q_all.astype(BF16)
    stat = m_sc.shape[1:]

    def scores(page, pg):
        sc = lax.dot_general(qb, page.astype(BF16), (((1,), (1,)), ((), ())),
                             preferred_element_type=F32)
        sc = sc + bias_ref[pg]
        m_pg = jnp.max(sc, axis=1, keepdims=True)
        p = jnp.exp(sc - m_pg)
        p_sc[pg] = p
        m_sc[pg] = jnp.broadcast_to(m_pg, stat)
        l_sc[pg] = jnp.broadcast_to(jnp.sum(p, axis=1, keepdims=True), stat)

    for p0 in range(0, n_pages, _PAGE_GROUP):
        consume(p0, scores)

    kn = kn_ref[0]
    sn = (jnp.sum(q_all * jnp.concatenate([kn, kn], axis=0), axis=1, keepdims=True)
          + bias0_ref[...])
    m_all = m_sc[...]
    m = jnp.maximum(jnp.max(m_all, axis=0), sn)
    e = jnp.exp(m_all - m[None])
    pn = jnp.exp(sn - m)
    l = jnp.sum(l_sc[...] * e, axis=0) + pn
    lam = _lambda(lv_ref, lam_init)
    r = jnp.concatenate([1.0 / l[:heads], lam / l[heads:]], axis=0)
    f_sc[...] = e * r[None]
    wn = pn * r

    reps = p_sc.shape[2] // LANES
    acc = (wn[:heads] - wn[heads:]) * vn_ref[0]
    def values(page, p):
        pg = p - n_pages
        pw = p_sc[pg] * jnp.tile(f_sc[pg], (1, reps))
        w = (pw[:heads] - pw[heads:]).astype(BF16)
        return _dot(w, page.astype(BF16))

    for p0 in range(n_pages, per_seq, _PAGE_GROUP):
        for part in consume(p0, values):
            acc = acc + part

    ms = jnp.mean(acc * acc, axis=-1, keepdims=True)
    o_ref[0] = acc * lax.rsqrt(ms + EPS) * g_ref[...] * (1.0 - lam_init)


def _attn_decode(q, k_new, v_new, cache_k, cache_v, layer, page_table, lam_vecs, rel_bias,
                 subln_g, lam_init):
    bd, width = q.shape
    heads = width // DV
    n_pages = page_table.shape[1]
    past = n_pages * PAGE_SIZE
    assert (2 * n_pages) % _PAGE_SLOTS == 0, "ring slot of a stream page must not depend on b"
    assert n_pages % _PAGE_GROUP == 0 and _PAGE_SLOTS % _PAGE_GROUP == 0
    depth, n_pool = cache_k.shape[:2]
    ck = cache_k.reshape(depth * n_pool, PAGE_SIZE * heads, DV)
    cv = cache_v.reshape(depth * n_pool, PAGE_SIZE * heads, DV)
    page0 = layer * n_pool
    bdist = _bias_by_distance(rel_bias, past + 1)
    b_past = bdist[1:][::-1].reshape(n_pages, 1, PAGE_SIZE, heads)
    own_head = jnp.eye(heads, dtype=bool)[None, :, None, :]
    b_past = jnp.where(own_head, b_past, -jnp.inf).reshape(n_pages, heads, PAGE_SIZE * heads)
    b_past = jnp.concatenate([b_past, b_past], axis=1)
    b_new = jnp.broadcast_to(jnp.tile(bdist[0], 2)[:, None], (2 * heads, LANES))

    row3 = lambda b, pt: (b, 0, 0)
    const2 = lambda b, pt: (0, 0)
    page_rows = PAGE_SIZE * heads
    in_specs = [pl.BlockSpec((4, DQK), const2),
                pl.BlockSpec((1, heads, DV), row3),
                pl.BlockSpec((1, heads, DV), row3),
                pl.BlockSpec((1, heads, DV), row3),
                pl.BlockSpec((n_pages, 2 * heads, page_rows), lambda b, pt: (0, 0, 0),
                             pipeline_mode=pl.Buffered(1)),
                pl.BlockSpec((2 * heads, LANES), const2),
                pl.BlockSpec((1, DV), const2),
                pl.BlockSpec(memory_space=pl.ANY),
                pl.BlockSpec(memory_space=pl.ANY)]
    body = functools.partial(_attn_decode_body, n_pages=n_pages, n_seq=bd, heads=heads,
                             page0=page0, lam_init=lam_init)
    out = pl.pallas_call(
        body,
        grid_spec=pltpu.PrefetchScalarGridSpec(
            num_scalar_prefetch=1,
            grid=(bd,),
            in_specs=in_specs,
            out_specs=pl.BlockSpec((1, heads, DV), row3),
            scratch_shapes=[pltpu.VMEM((_PAGE_SLOTS, page_rows, DV), F32),
                            pltpu.SemaphoreType.DMA((_PAGE_SLOTS,)),
                            pltpu.VMEM((n_pages, 2 * heads, page_rows), F32),
                            pltpu.VMEM((n_pages, 2 * heads, LANES), F32),
                            pltpu.VMEM((n_pages, 2 * heads, LANES), F32),
                            pltpu.VMEM((n_pages, 2 * heads, LANES), F32)]),
        out_shape=jax.ShapeDtypeStruct((bd, heads, DV), F32),
        compiler_params=_params("arbitrary"),
        name="attn_decode",
    )(page_table, lam_vecs, q.reshape(bd, heads, DV), k_new.reshape(bd, heads, DV),
      v_new.reshape(bd, heads, DV), b_past, b_new, subln_g.reshape(1, DV), ck, cv)
    return out.reshape(bd, width)


def _topk_rows(s, k, ties):
    rows = s.shape[0]
    rank = jnp.full(s.shape, float(k), F32)
    if ties:
        iota = lax.broadcasted_iota(jnp.int32, s.shape, 0).astype(F32)
    vals = []
    for p in range(k):
        m = jnp.max(s, axis=0, keepdims=True)
        sel = s == m
        if ties:
            idx = jnp.min(jnp.where(sel, iota, float(rows)), axis=0, keepdims=True)
            sel = iota == idx
        rank = jnp.where(sel, float(p), rank)
        s = jnp.where(sel, -jnp.inf, s)
        vals.append(m)
    return vals, rank


def _route_tables(s1, s2, ties):
    k = PEER_TOPK
    v1, rank1 = _topk_rows(s1, k, ties)
    v2, rank2 = _topk_rows(s2, k, ties)
    v1_all = jnp.concatenate(v1, axis=0)
    v2_all = jnp.concatenate(v2, axis=0)
    sub = lax.broadcasted_iota(jnp.int32, (8, s1.shape[1]), 0)
    blocks, spans, row = [], [], 0
    p = 0
    while p < k:
        nq = k // (p + 1)
        if nq >= 8:
            nb = -(-nq // 8) * 8
            blk = v1[p] + v2_all[0:nb]
            if nb != nq:
                blk = jnp.where(jnp.concatenate([sub + 8 * i for i in range(nb // 8)]) < nq,
                                blk, -jnp.inf)
            blocks.append(blk)
            spans.append((p, row, nb))
            row += nb
            p += 1
        elif nq > 1:
            blocks.append(jnp.where(sub < nq, v1[p] + v2_all[0:8], -jnp.inf))
            spans.append((p, row, 8))
            row += 8
            p += 1
        else:
            assert (k - p) == 8, "single-candidate rows are packed as one aligned 8-row block"
            blocks.append(v1_all[p:p + 8] + v2[0])
            for r in range(8):
                spans.append((p + r, row + r, 1))
            row += 8
            p += 8
    cand = jnp.concatenate(blocks, axis=0)
    top, rank_c = _topk_rows(cand, k, ties)
    chosen = jnp.where(rank_c < float(k), 1.0, 0.0)
    z = jnp.zeros_like(top[0])
    for i in range(k):
        z = z + jnp.exp(top[i] - top[0])
    n1 = jnp.zeros(s1.shape, F32)
    for pp, r0, nr in spans:
        n_p = jnp.sum(chosen[r0:r0 + nr], axis=0, keepdims=True)
        n1 = jnp.where(rank1 == float(pp), n_p, n1)
    marked = (jnp.sum(jnp.where(rank1 < float(k), 1.0, 0.0), axis=0, keepdims=True)
              + jnp.sum(jnp.where(rank2 < float(k), 1.0, 0.0), axis=0, keepdims=True)
              + jnp.sum(chosen, axis=0, keepdims=True))
    tables = (n1, jnp.exp(s1 - v1[0]) / z, rank2, jnp.exp(s2 - v2[0]))
    return tables, marked - 3.0 * k


def _route_body(qr_ref, keys_ref, n1_ref, w1_ref, r2_ref, w2_ref):
    nt = (((1,), (1,)), ((), ()))
    s1 = lax.dot_general(keys_ref[0], qr_ref[:, 0:LANES], nt, preferred_element_type=F32)
    s2 = lax.dot_general(keys_ref[1], qr_ref[:, LANES:2 * LANES], nt, preferred_element_type=F32)
    out_refs = (n1_ref, w1_ref, r2_ref, w2_ref)
    tables, extra = _route_tables(s1, s2, ties=False)
    tied = jnp.max(extra) > 0.0
    for ref, t in zip(out_refs, tables):
        ref[0] = t

    @pl.when(tied)
    def _exact():
        exact_tables, _ = _route_tables(s1, s2, ties=True)
        for ref, t in zip(out_refs, exact_tables):
            ref[0] = t


def _route(qr, sub_keys, tt):
    n = qr.shape[0]
    heads, _, n_keys, d_half = sub_keys.shape
    keys = sub_keys.reshape(heads * 2, n_keys, d_half).astype(BF16)
    out = jax.ShapeDtypeStruct((heads, n_keys, n), F32)
    ospec = pl.BlockSpec((1, n_keys, tt), lambda i, h: (h, 0, i))
    return pl.pallas_call(
        _route_body,
        grid=(n // tt, heads),
        in_specs=[pl.BlockSpec((tt, 2 * d_half), lambda i, h: (i, h)),
                  pl.BlockSpec((2, n_keys, d_half), lambda i, h: (h, 0, 0))],
        out_specs=[ospec, ospec, ospec, ospec],
        out_shape=[out, out, out, out],
        compiler_params=_params("parallel", "arbitrary"),
        name="peer_route",
    )(qr, keys)


def _gelu(x):
    return 0.5 * x * (1.0 + lax.erf(x * (2.0 ** -0.5)))


def _peer_body(h_ref, u_ref, v_ref, n1_ref, w1_ref, r2_ref, w2_ref, x_ref, o_ref,
               z0_sc, z1_sc, a_sc, *, ec, tt, heads, n_keys):
    j = pl.program_id(1)
    nsl = ec // n_keys

    @pl.when(j == 0)
    def _init():
        o_ref[...] = x_ref[...]
        z1_sc[...] = jnp.zeros(z1_sc.shape, F32)

    def stage(z_next, z_prev):
        z_next[...] = lax.dot_general(u_ref[...], h_ref[...], (((1,), (1,)), ((), ())),
                                      preferred_element_type=F32)
        for sl in range(nsl):
            for tb in range(tt // LANES):
                cols = slice(tb * LANES, (tb + 1) * LANES)
                g = jnp.zeros((n_keys, LANES), F32)
                for h in range(heads):
                    row = h * nsl + sl
                    n1 = n1_ref[0, row:row + 1, cols]
                    w1 = w1_ref[0, row:row + 1, cols]
                    g = g + jnp.where(r2_ref[h, :, cols] < n1, w2_ref[h, :, cols] * w1, 0.0)
                z = z_prev[sl * n_keys:(sl + 1) * n_keys, cols]
                a_sc[sl * n_keys:(sl + 1) * n_keys, cols] = (_gelu(z) * g).astype(a_sc.dtype)
        o_ref[...] += lax.dot_general(a_sc[...], v_ref[...], (((0,), (0,)), ((), ())),
                                      preferred_element_type=F32)

    @pl.when(j % 2 == 0)
    def _even():
        stage(z0_sc, z1_sc)

    @pl.when(j % 2 == 1)
    def _odd():
        stage(z1_sc, z0_sc)


def _peer(h2, u_bf, v_bf, tables, x, tt, ec):
    n, d = h2.shape
    n_exp = u_bf.shape[0]
    n1, w1, r2, w2 = tables
    heads, n_keys, _ = n1.shape
    nsl = ec // n_keys
    nc = n_exp // ec

    def by_chunk(t):
        t = t.reshape(heads, n_keys // nsl, nsl, n)
        return jnp.transpose(t, (1, 0, 2, 3)).reshape(n_keys // nsl, heads * nsl, n)

    once = pl.Buffered(1)
    tspec = pl.BlockSpec((heads, n_keys, tt), lambda i, j: (0, 0, i), pipeline_mode=once)
    prev = lambda j: jnp.maximum(j - 1, 0)
    cspec = pl.BlockSpec((1, heads * nsl, tt), lambda i, j: (prev(j), 0, i))
    body = functools.partial(_peer_body, ec=ec, tt=tt, heads=heads, n_keys=n_keys)
    return pl.pallas_call(
        body,
        grid=(n // tt, nc + 1),
        in_specs=[pl.BlockSpec((tt, d), lambda i, j: (i, 0), pipeline_mode=once),
                  pl.BlockSpec((ec, d), lambda i, j: (jnp.minimum(j, nc - 1), 0)),
                  pl.BlockSpec((ec, d), lambda i, j: (prev(j), 0)),
                  cspec, cspec, tspec, tspec,
                  pl.BlockSpec((tt, d), lambda i, j: (i, 0), pipeline_mode=once)],
        out_specs=pl.BlockSpec((tt, d), lambda i, j: (i, 0)),
        out_shape=jax.ShapeDtypeStruct((n, d), F32),
        scratch_shapes=[pltpu.VMEM((ec, tt), F32), pltpu.VMEM((ec, tt), F32),
                        pltpu.VMEM((ec, tt), BF16)],
        compiler_params=_params("parallel", "arbitrary"),
        name="peer_experts",
    )(h2, u_bf, v_bf, by_chunk(n1), by_chunk(w1), r2, w2, x)


def _tile(n, pref):
    return pref if n % pref == 0 else n


def _project(x2d, ln_g, w_in_bf, qn_g, kn_g, widths):
    pool_w, attn_w = widths
    n, d = x2d.shape
    tm = _tile(n, 1024)
    xn = _rmsnorm(x2d, ln_g, _tile(n, 256))
    mm = functools.partial(_matmul, [xn], [(w_in_bf, 0, d)], tm=tm, tn=512, out_dtype=F32)
    u = mm(n_out=pool_w, w_col_off=0, name="proj_u")
    q = mm(n_out=attn_w, w_col_off=pool_w, gain=qn_g, qk_group=DQK, name="proj_q")
    k = mm(n_out=attn_w, w_col_off=pool_w + attn_w, gain=kn_g, qk_group=DQK, name="proj_k")
    v = mm(n_out=attn_w, w_col_off=pool_w + 2 * attn_w, name="proj_v")
    return u, q, k, v


def _mix_and_peer(x2d, pool_y, att, w_o_bf, ln2_g, w_query_bf, sub_keys, u_bf, v_bf):
    n, d = x2d.shape
    pool_w = pool_y.shape[1]
    tm = _tile(n, 1024)
    assert att.shape[1] == pool_w, "w_o row blocks assume equal pool and attention widths"
    x1 = _matmul([pool_y, att], [(w_o_bf, 0, pool_w), (w_o_bf, 1, pool_w)],
                 n_out=d, tm=tm, tn=512, out_dtype=F32, res=x2d, name="out_proj")
    h2 = _rmsnorm(x1, ln2_g, _tile(n, 256))
    qr = _matmul([h2], [(w_query_bf, 0, d)], n_out=w_query_bf.shape[1], tm=tm, tn=512,
                 out_dtype=BF16, name="peer_query")
    tables = _route(qr, sub_keys, _tile(n, LANES))
    return _peer(h2, u_bf, v_bf, tables, x1, _tile(n, 512), 512)


def kernel(x_prompt, x_sample, cache_k, cache_v, state_pool, page_table, rel_bias, ln1_g, w_in, q_norm_g, k_norm_g, lambda_q1, lambda_k1, lambda_q2, lambda_k2, subln_g, w_pool, pool_scale, w_o, ln2_g, w_query, sub_keys, peer_u, peer_v):
    batch, seq, d = x_prompt.shape
    bd, tq, _ = x_sample.shape
    assert tq == 1, "sample group is one new position per sequence"
    depth = w_in.shape[0]
    pool_w = w_pool.shape[1] * w_pool.shape[2]
    attn_w = (w_in.shape[2] - pool_w) // 3
    heads = attn_w // DV
    n_state = state_pool.shape[2]

    xp = x_prompt.reshape(batch * seq, d)
    xs = x_sample.reshape(bd, d)
    kp_l, vp_l, pp_l, ks_l, vs_l, ps_l = [], [], [], [], [], []
    for l in range(depth):
        lam_init = 0.8 - 0.6 * math.exp(-0.3 * l)
        lam_vecs = jnp.stack([lambda_q1[l], lambda_k1[l], lambda_q2[l], lambda_k2[l]]).astype(F32)
        w_in_bf = w_in[l].astype(BF16)
        w_o_bf = w_o[l].astype(BF16)
        w_query_bf = w_query[l].astype(BF16)
        u_bf = peer_u[l].astype(BF16)
        v_bf = peer_v[l].astype(BF16)

        u, q, k, v = _project(xp, ln1_g[l], w_in_bf, q_norm_g[l], k_norm_g[l], (pool_w, attn_w))
        pool_y = _pool_prompt(u, w_pool[l], pool_scale[l], batch, seq)
        att = _attn_prompt(q, k, v, lam_vecs, rel_bias, subln_g[l], batch, seq, lam_init)
        xp = _mix_and_peer(xp, pool_y, att, w_o_bf, ln2_g[l], w_query_bf, sub_keys[l], u_bf, v_bf)
        kp_l.append(k.reshape(batch, seq, heads, 2 * DQK))
        vp_l.append(v.reshape(batch, seq, heads, DV))
        pp_l.append(u.reshape(batch, seq, pool_w)[:, seq - n_state:])

        u, q, k, v = _project(xs, ln1_g[l], w_in_bf, q_norm_g[l], k_norm_g[l], (pool_w, attn_w))
        pool_y = _pool_sample(state_pool[l], u, w_pool[l], pool_scale[l])
        att = _attn_decode(q, k, v, cache_k, cache_v, l, page_table, lam_vecs, rel_bias,
                           subln_g[l], lam_init).astype(BF16)
        xs = _mix_and_peer(xs, pool_y, att, w_o_bf, ln2_g[l], w_query_bf, sub_keys[l], u_bf, v_bf)
        ks_l.append(k.reshape(bd, tq, heads, 2 * DQK))
        vs_l.append(v.reshape(bd, tq, heads, DV))
        ps_l.append(jnp.concatenate([state_pool[l], u[:, None, :]], axis=1)[:, -n_state:])

    return (xp.reshape(batch, seq, d), xs.reshape(bd, tq, d),
            jnp.stack(kp_l), jnp.stack(vp_l), jnp.stack(pp_l),
            jnp.stack(ks_l), jnp.stack(vs_l), jnp.stack(ps_l))
```

```python
import functools
import math

import jax
import jax.numpy as jnp
from jax import lax
from jax.experimental import pallas as pl
from jax.experimental.pallas import tpu as pltpu

F32 = jnp.float32
BF16 = jnp.bfloat16
EPS = 1e-6

LANES = 128
VMEM_LIMIT_BYTES = 56 * 1024 * 1024

POOL_WINDOWS = (2, 4, 8, 16)
DQK = 64
DV = 128
NUM_BUCKETS = 32
MAX_DISTANCE = 128
PEER_TOPK = 16
PAGE_SIZE = 128


def _params(*sem, flags=None):
    return pltpu.CompilerParams(dimension_semantics=sem, vmem_limit_bytes=VMEM_LIMIT_BYTES,
                                flags=flags)


def _split_bf16(x):
    hi = x.astype(BF16)
    lo = (x - hi.astype(F32)).astype(BF16)
    return hi, lo


def _dot(a, b):
    return jnp.dot(a, b, preferred_element_type=F32)


def _dot2(x, w):
    hi, lo = _split_bf16(x)
    return _dot(hi, w) + _dot(lo, w)


def _rmsnorm_body(x_ref, g_ref, o_ref):
    x = x_ref[...]
    ms = jnp.mean(x * x, axis=-1, keepdims=True)
    o_ref[...] = (x * lax.rsqrt(ms + EPS) * g_ref[...]).astype(o_ref.dtype)


def _rmsnorm(x, g, tm):
    m, d = x.shape
    return pl.pallas_call(
        _rmsnorm_body,
        grid=(m // tm,),
        in_specs=[pl.BlockSpec((tm, d), lambda i: (i, 0)),
                  pl.BlockSpec((1, d), lambda i: (0, 0))],
        out_specs=pl.BlockSpec((tm, d), lambda i: (i, 0)),
        out_shape=jax.ShapeDtypeStruct((m, d), BF16),
        compiler_params=_params("parallel"),
        name="rmsnorm",
    )(x, g.reshape(1, d))


def _mm_body(*refs, n_pairs, qk_group, has_res):
    a_refs = refs[:n_pairs]
    w_refs = refs[n_pairs:2 * n_pairs]
    pos = 2 * n_pairs
    if qk_group:
        g_ref, ones_ref = refs[pos], refs[pos + 1]
        pos += 2
    if has_res:
        r_ref = refs[pos]
        pos += 1
    o_ref = refs[pos]

    acc = _dot(a_refs[0][...], w_refs[0][...])
    for p in range(1, n_pairs):
        acc = acc + _dot(a_refs[p][...], w_refs[p][...])
    if qk_group:
        sq = acc * acc
        hi, lo = _split_bf16(sq)
        ones = ones_ref[...]
        parts = []
        for s in range(acc.shape[1] // LANES):
            sl = slice(s * LANES, (s + 1) * LANES)
            parts.append(_dot(hi[:, sl], ones) + _dot(lo[:, sl], ones))
        gs = jnp.concatenate(parts, axis=1)
        acc = acc * lax.rsqrt(gs * (1.0 / qk_group) + EPS) * g_ref[...]
    if has_res:
        acc = acc + r_ref[...]
    o_ref[...] = acc.astype(o_ref.dtype)


def _matmul(a_list, w_list, *, n_out, tm, tn, out_dtype, w_col_off=0,
            gain=None, qk_group=0, res=None, name="matmul"):
    m = a_list[0].shape[0]
    tn = _tile(n_out, tn)
    off = w_col_off // tn
    in_specs, args = [], []
    for a in a_list:
        in_specs.append(pl.BlockSpec((tm, a.shape[1]), lambda i, j: (i, 0)))
        args.append(a)
    for w, rb, kr in w_list:
        in_specs.append(pl.BlockSpec((kr, tn), lambda i, j, rb=rb: (rb, j + off)))
        args.append(w)
    if qk_group:
        lane = jnp.arange(LANES)
        ones_bd = (lane[:, None] // qk_group == lane[None, :] // qk_group).astype(BF16)
        in_specs += [pl.BlockSpec((1, tn), lambda i, j: (0, 0)),
                     pl.BlockSpec((LANES, LANES), lambda i, j: (0, 0))]
        args += [jnp.tile(gain.astype(F32), tn // gain.shape[0]).reshape(1, tn), ones_bd]
    if res is not None:
        in_specs.append(pl.BlockSpec((tm, tn), lambda i, j: (i, j)))
        args.append(res)
    body = functools.partial(_mm_body, n_pairs=len(a_list), qk_group=qk_group,
                             has_res=res is not None)
    return pl.pallas_call(
        body,
        grid=(m // tm, n_out // tn),
        in_specs=in_specs,
        out_specs=pl.BlockSpec((tm, tn), lambda i, j: (i, j)),
        out_shape=jax.ShapeDtypeStruct((m, n_out), out_dtype),
        compiler_params=_params("parallel", "arbitrary"),
        name=name,
    )(*args)


_POOL_PAD = 16
_POOL_ROWS = 256


def _pool_prompt_body(u_ref, w_ref, sc_ref, o_ref, pad_ref, *, seq):
    g = pl.program_id(1)
    pad_ref[0:_POOL_PAD, :] = jnp.zeros((_POOL_PAD, pad_ref.shape[1]), F32)
    pad_ref[_POOL_PAD:, :] = u_ref[...]
    wmat = w_ref[0].astype(BF16)
    scale = sc_ref[...]
    for gi, win in enumerate(POOL_WINDOWS):
        @pl.when(g == gi)
        def _(win=win):
            for r0 in range(0, seq, _POOL_ROWS):
                cur = pad_ref[r0 + _POOL_PAD:r0 + _POOL_PAD + _POOL_ROWS, :]
                tot = cur
                for i in range(1, win):
                    tot = tot + pad_ref[r0 + _POOL_PAD - i:r0 + _POOL_PAD - i + _POOL_ROWS, :]
                pos = r0 + lax.broadcasted_iota(jnp.int32, (_POOL_ROWS, 1), 0)
                cnt = jnp.minimum(win, pos + 1).astype(F32)
                d = tot / cnt - cur
                y = _dot(d.astype(BF16), wmat) * scale
                o_ref[r0:r0 + _POOL_ROWS, :] = y.astype(o_ref.dtype)


def _pool_prompt(u, w_pool, pool_scale, batch, seq):
    n, width = u.shape
    ng = len(POOL_WINDOWS)
    gw = width // ng
    body = functools.partial(_pool_prompt_body, seq=seq)
    return pl.pallas_call(
        body,
        grid=(batch, ng),
        in_specs=[pl.BlockSpec((seq, gw), lambda b, g: (b, g)),
                  pl.BlockSpec((1, gw, gw), lambda b, g: (g, 0, 0)),
                  pl.BlockSpec((1, gw), lambda b, g: (0, g))],
        out_specs=pl.BlockSpec((seq, gw), lambda b, g: (b, g)),
        out_shape=jax.ShapeDtypeStruct((n, width), BF16),
        scratch_shapes=[pltpu.VMEM((seq + _POOL_PAD, gw), F32)],
        compiler_params=_params("parallel", "arbitrary"),
        name="pool_prompt",
    )(u, w_pool, pool_scale.reshape(1, width))


def _pool_sample_body(st_ref, u_ref, w_ref, sc_ref, o_ref, *, n_state, width):
    ng = len(POOL_WINDOWS)
    gw = width // ng
    for gi, win in enumerate(POOL_WINDOWS):
        c0 = gi * gw
        cur = u_ref[:, c0:c0 + gw]
        tot = cur
        for i in range(1, win):
            r = n_state - i
            tot = tot + st_ref[:, r * width + c0:r * width + c0 + gw]
        d = tot / float(win) - cur
        y = _dot(d.astype(BF16), w_ref[gi].astype(BF16)) * sc_ref[:, c0:c0 + gw]
        o_ref[:, c0:c0 + gw] = y.astype(o_ref.dtype)


def _pool_sample(state, u, w_pool, pool_scale):
    bd, n_state, width = state.shape
    body = functools.partial(_pool_sample_body, n_state=n_state, width=width)
    return pl.pallas_call(
        body,
        out_shape=jax.ShapeDtypeStruct((bd, width), BF16),
        compiler_params=pltpu.CompilerParams(vmem_limit_bytes=VMEM_LIMIT_BYTES),
        name="pool_sample",
    )(state.reshape(bd, n_state * width), u, w_pool, pool_scale.reshape(1, width))


def _rel_bucket(n):
    max_exact = NUM_BUCKETS // 2
    nf = jnp.maximum(n, 1).astype(F32)
    large = max_exact + (jnp.log(nf / max_exact) / math.log(MAX_DISTANCE / max_exact)
                         * (NUM_BUCKETS - max_exact)).astype(jnp.int32)
    large = jnp.minimum(large, NUM_BUCKETS - 1)
    return jnp.where(n < max_exact, n, large)


def _bias_by_distance(rel_bias, n_dist):
    return rel_bias.astype(F32)[_rel_bucket(jnp.arange(n_dist))]


def _lambda(lv_ref, lam_init):
    lv = lv_ref[...]
    a = jnp.sum(lv[0:1] * lv[1:2], axis=-1, keepdims=True)
    b = jnp.sum(lv[2:3] * lv[3:4], axis=-1, keepdims=True)
    return jnp.exp(a) - jnp.exp(b) + lam_init


def _attn_prompt_body(lv_ref, q_ref, k_ref, v_ref, bias_ref, g_ref, o_ref,
                      vt_sc, m_sc, l_sc, acc_sc, *, tq, tk, seq, sb, lam_init):
    qi = pl.program_id(2)

    @pl.when(qi == 0)
    def _transpose_values():
        for c in range(seq // tk):
            vt_sc[c] = v_ref[c * tk:(c + 1) * tk, :].T.astype(BF16)

    q = q_ref[...] * (DQK ** -0.5)
    lane = lax.broadcasted_iota(jnp.int32, q.shape, 1)
    q_ext = jnp.concatenate([jnp.where(lane < DQK, q, 0.0),
                             jnp.where(lane >= DQK, q, 0.0)], axis=0).astype(BF16)
    m_sc[...] = jnp.full(m_sc.shape, -jnp.inf, F32)
    l_sc[...] = jnp.zeros(l_sc.shape, F32)
    acc_sc[...] = jnp.zeros(acc_sc.shape, F32)

    def step(j, carry):
        r0 = pl.multiple_of(j * tk, tk)
        kj = k_ref[pl.ds(r0, tk), :].astype(BF16)
        st = lax.dot_general(kj, q_ext, (((1,), (1,)), ((), ())),
                             preferred_element_type=F32)
        d0 = (qi * tq - j * tk) // sb
        rows = []
        for a in range(tk // sb):
            row = [bias_ref[0, jnp.clip(d0 + b - a + 1, 0, 3)] for b in range(tq // sb)]
            rows.append(jnp.concatenate(row + row, axis=1))
        st = st + jnp.concatenate(rows, axis=0)
        m_old = m_sc[...]
        m_new = jnp.maximum(m_old, jnp.max(st, axis=0, keepdims=True))
        alpha = jnp.exp(m_old - m_new)
        p = jnp.exp(st - m_new)
        l_sc[...] = alpha * l_sc[...] + jnp.sum(p, axis=0, keepdims=True)
        acc_sc[...] = alpha * acc_sc[...] + _dot(vt_sc[j], p.astype(BF16))
        m_sc[...] = m_new
        return carry

    lax.fori_loop(0, (qi * tq) // tk + 1, step, 0)

    lam = _lambda(lv_ref, lam_init)
    o = acc_sc[...] / l_sc[...]
    att = (o[:, :tq] - lam * o[:, tq:]).T
    ms = jnp.mean(att * att, axis=-1, keepdims=True)
    out = att * lax.rsqrt(ms + EPS) * g_ref[...] * (1.0 - lam_init)
    o_ref[...] = out.astype(o_ref.dtype)


def _toeplitz(g, rows, cols):
    h = g.shape[0]
    period = rows + cols
    gp = jnp.pad(g, ((0, 0), (0, 1)))
    flat = jnp.tile(gp, (1, rows))[:, :rows * (period - 1)]
    return flat.reshape(h, rows, period - 1)[:, :, rows - 1:rows - 1 + cols]


def _attn_prompt(q, k, v, lam_vecs, rel_bias, subln_g, batch, seq, lam_init, tq=512, tk=512):
    n, width = q.shape
    heads = width // DV
    nq = seq // tq
    sb = MAX_DISTANCE
    assert seq % tk == 0 and tk % tq == 0 and tq % sb == 0 and sb == LANES
    bd = _bias_by_distance(rel_bias, 2 * sb).T
    line = jnp.concatenate([jnp.full((heads, sb - 1), -jnp.inf, F32), bd], axis=1)
    blocks = [jnp.full((heads, sb, sb), -jnp.inf, F32)]
    blocks += [_toeplitz(line[:, d * sb:d * sb + 2 * sb - 1], sb, sb) for d in range(2)]
    blocks.append(jnp.broadcast_to(bd[:, 2 * sb - 1][:, None, None], (heads, sb, sb)))
    bias_tiles = jnp.stack(blocks, axis=1)
    body = functools.partial(_attn_prompt_body, tq=tq, tk=tk, seq=seq, sb=sb,
                             lam_init=lam_init)
    return pl.pallas_call(
        body,
        grid=(batch, heads, nq),
        in_specs=[pl.BlockSpec((4, DQK), lambda b, h, i: (0, 0)),
                  pl.BlockSpec((tq, DV), lambda b, h, i: (b * nq + i, h)),
                  pl.BlockSpec((seq, DV), lambda b, h, i: (b, h)),
                  pl.BlockSpec((seq, DV), lambda b, h, i: (b, h)),
                  pl.BlockSpec((1, 4, sb, sb), lambda b, h, i: (h, 0, 0, 0)),
                  pl.BlockSpec((1, DV), lambda b, h, i: (0, 0))],
        out_specs=pl.BlockSpec((tq, DV), lambda b, h, i: (b * nq + i, h)),
        out_shape=jax.ShapeDtypeStruct((n, width), BF16),
        scratch_shapes=[pltpu.VMEM((seq // tk, DV, tk), BF16),
                        pltpu.VMEM((1, 2 * tq), F32), pltpu.VMEM((1, 2 * tq), F32),
                        pltpu.VMEM((DV, 2 * tq), F32)],
        compiler_params=_params("parallel", "parallel", "arbitrary"),
        name="attn_prompt",
    )(lam_vecs, q, k, v, bias_tiles, subln_g.reshape(1, DV))


_PAGE_SLOTS = 16
_PAGE_GROUP = 4


def _attn_decode_body(pt_ref, lv_ref, q_ref, kn_ref, vn_ref, bias_ref, bias0_ref, g_ref,
                      ck_hbm, cv_hbm, o_ref, ring, sem, p_sc, m_sc, l_sc, f_sc,
                      *, n_pages, n_seq, heads, page0, lam_init):
    b = pl.program_id(0)
    per_seq = 2 * n_pages

    def page_copy(seq, p):
        src = ck_hbm if p < n_pages else cv_hbm
        slot = p % _PAGE_SLOTS
        pid = page0 + pt_ref[seq, p % n_pages]
        return pltpu.make_async_copy(src.at[pid], ring.at[slot], sem.at[slot])

    @pl.when(b == 0)
    def _prime():
        for p in range(_PAGE_SLOTS):
            page_copy(0, p).start()

    def consume(p0, fn):
        group = range(p0, p0 + _PAGE_GROUP)
        for p in group:
            page_copy(b, p).wait()
        outs = [fn(ring[p % _PAGE_SLOTS], p) for p in group]
        for p in group:
            nxt = p + _PAGE_SLOTS
            if nxt < per_seq:
                page_copy(b, nxt).start()
            else:
                @pl.when(b + 1 < n_seq)
                def _next_seq(nxt=nxt):
                    page_copy(b + 1, nxt - per_seq).start()
        return outs

    q = q_ref[0] * (DQK ** -0.5)
    lane = lax.broadcasted_iota(jnp.int32, q.shape, 1)
    q_all = jnp.concatenate([jnp.where(lane < DQK, q, 0.0),
                             jnp.where(lane >= DQK, q, 0.0)], axis=0)
    qb = q_all.astype(BF16)
    stat = m_sc.shape[1:]

    def scores(page, pg):
        sc = lax.dot_general(qb, page.astype(BF16), (((1,), (1,)), ((), ())),
                             preferred_element_type=F32)
        sc = sc + bias_ref[pg]
        m_pg = jnp.max(sc, axis=1, keepdims=True)
        p = jnp.exp(sc - m_pg)
        p_sc[pg] = p
        m_sc[pg] = jnp.broadcast_to(m_pg, stat)
        l_sc[pg] = jnp.broadcast_to(jnp.sum(p, axis=1, keepdims=True), stat)

    for p0 in range(0, n_pages, _PAGE_GROUP):
        consume(p0, scores)

    kn = kn_ref[0]
    sn = (jnp.sum(q_all * jnp.concatenate([kn, kn], axis=0), axis=1, keepdims=True)
          + bias0_ref[...])
    m_all = m_sc[...]
    m = jnp.maximum(jnp.max(m_all, axis=0), sn)
    e = jnp.exp(m_all - m[None])
    pn = jnp.exp(sn - m)
    l = jnp.sum(l_sc[...] * e, axis=0) + pn
    lam = _lambda(lv_ref, lam_init)
    r = jnp.concatenate([1.0 / l[:heads], lam / l[heads:]], axis=0)
    f_sc[...] = e * r[None]
    wn = pn * r

    reps = p_sc.shape[2] // LANES
    acc = (wn[:heads] - wn[heads:]) * vn_ref[0]
    def values(page, p):
        pg = p - n_pages
        pw = p_sc[pg] * jnp.tile(f_sc[pg], (1, reps))
        w = (pw[:heads] - pw[heads:]).astype(BF16)
        return _dot(w, page.astype(BF16))

    for p0 in range(n_pages, per_seq, _PAGE_GROUP):
        for part in consume(p0, values):
            acc = acc + part

    ms = jnp.mean(acc * acc, axis=-1, keepdims=True)
    o_ref[0] = acc * lax.rsqrt(ms + EPS) * g_ref[...] * (1.0 - lam_init)


def _attn_decode(q, k_new, v_new, cache_k, cache_v, layer, page_table, lam_vecs, rel_bias,
                 subln_g, lam_init):
    bd, width = q.shape
    heads = width // DV
    n_pages = page_table.shape[1]
    past = n_pages * PAGE_SIZE
    assert (2 * n_pages) % _PAGE_SLOTS == 0, "ring slot of a stream page must not depend on b"
    assert n_pages % _PAGE_GROUP == 0 and _PAGE_SLOTS % _PAGE_GROUP == 0
    depth, n_pool = cache_k.shape[:2]
    ck = cache_k.reshape(depth * n_pool, PAGE_SIZE * heads, DV)
    cv = cache_v.reshape(depth * n_pool, PAGE_SIZE * heads, DV)
    page0 = layer * n_pool
    bdist = _bias_by_distance(rel_bias, past + 1)
    b_past = bdist[1:][::-1].reshape(n_pages, 1, PAGE_SIZE, heads)
    own_head = jnp.eye(heads, dtype=bool)[None, :, None, :]
    b_past = jnp.where(own_head, b_past, -jnp.inf).reshape(n_pages, heads, PAGE_SIZE * heads)
    b_past = jnp.concatenate([b_past, b_past], axis=1)
    b_new = jnp.broadcast_to(jnp.tile(bdist[0], 2)[:, None], (2 * heads, LANES))

    row3 = lambda b, pt: (b, 0, 0)
    const2 = lambda b, pt: (0, 0)
    page_rows = PAGE_SIZE * heads
    in_specs = [pl.BlockSpec((4, DQK), const2),
                pl.BlockSpec((1, heads, DV), row3),
                pl.BlockSpec((1, heads, DV), row3),
                pl.BlockSpec((1, heads, DV), row3),
                pl.BlockSpec((n_pages, 2 * heads, page_rows), lambda b, pt: (0, 0, 0),
                             pipeline_mode=pl.Buffered(1)),
                pl.BlockSpec((2 * heads, LANES), const2),
                pl.BlockSpec((1, DV), const2),
                pl.BlockSpec(memory_space=pl.ANY),
                pl.BlockSpec(memory_space=pl.ANY)]
    body = functools.partial(_attn_decode_body, n_pages=n_pages, n_seq=bd, heads=heads,
                             page0=page0, lam_init=lam_init)
    out = pl.pallas_call(
        body,
        grid_spec=pltpu.PrefetchScalarGridSpec(
            num_scalar_prefetch=1,
            grid=(bd,),
            in_specs=in_specs,
            out_specs=pl.BlockSpec((1, heads, DV), row3),
            scratch_shapes=[pltpu.VMEM((_PAGE_SLOTS, page_rows, DV), F32),
                            pltpu.SemaphoreType.DMA((_PAGE_SLOTS,)),
                            pltpu.VMEM((n_pages, 2 * heads, page_rows), F32),
                            pltpu.VMEM((n_pages, 2 * heads, LANES), F32),
                            pltpu.VMEM((n_pages, 2 * heads, LANES), F32),
                            pltpu.VMEM((n_pages, 2 * heads, LANES), F32)]),
        out_shape=jax.ShapeDtypeStruct((bd, heads, DV), F32),
        compiler_params=_params("arbitrary"),
        name="attn_decode",
    )(page_table, lam_vecs, q.reshape(bd, heads, DV), k_new.reshape(bd, heads, DV),
      v_new.reshape(bd, heads, DV), b_past, b_new, subln_g.reshape(1, DV), ck, cv)
    return out.reshape(bd, width)


def _topk_rows(s, k, ties):
    rows = s.shape[0]
    rank = jnp.full(s.shape, float(k), F32)
    if ties:
        iota = lax.broadcasted_iota(jnp.int32, s.shape, 0).astype(F32)
    vals = []
    for p in range(k):
        m = jnp.max(s, axis=0, keepdims=True)
        sel = s == m
        if ties:
            idx = jnp.min(jnp.where(sel, iota, float(rows)), axis=0, keepdims=True)
            sel = iota == idx
        rank = jnp.where(sel, float(p), rank)
        s = jnp.where(sel, -jnp.inf, s)
        vals.append(m)
    return vals, rank


def _route_tables(s1, s2, ties):
    k = PEER_TOPK
    v1, rank1 = _topk_rows(s1, k, ties)
    v2, rank2 = _topk_rows(s2, k, ties)
    v1_all = jnp.concatenate(v1, axis=0)
    v2_all = jnp.concatenate(v2, axis=0)
    sub = lax.broadcasted_iota(jnp.int32, (8, s1.shape[1]), 0)
    blocks, spans, row = [], [], 0
    p = 0
    while p < k:
        nq = k // (p + 1)
        if nq >= 8:
            nb = -(-nq // 8) * 8
            blk = v1[p] + v2_all[0:nb]
            if nb != nq:
                blk = jnp.where(jnp.concatenate([sub + 8 * i for i in range(nb // 8)]) < nq,
                                blk, -jnp.inf)
            blocks.append(blk)
            spans.append((p, row, nb))
            row += nb
            p += 1
        elif nq > 1:
            blocks.append(jnp.where(sub < nq, v1[p] + v2_all[0:8], -jnp.inf))
            spans.append((p, row, 8))
            row += 8
            p += 1
        else:
            assert (k - p) == 8, "single-candidate rows are packed as one aligned 8-row block"
            blocks.append(v1_all[p:p + 8] + v2[0])
            for r in range(8):
                spans.append((p + r, row + r, 1))
            row += 8
            p += 8
    cand = jnp.concatenate(blocks, axis=0)
    top, rank_c = _topk_rows(cand, k, ties)
    chosen = jnp.where(rank_c < float(k), 1.0, 0.0)
    z = jnp.zeros_like(top[0])
    for i in range(k):
        z = z + jnp.exp(top[i] - top[0])
    n1 = jnp.zeros(s1.shape, F32)
    for pp, r0, nr in spans:
        n_p = jnp.sum(chosen[r0:r0 + nr], axis=0, keepdims=True)
        n1 = jnp.where(rank1 == float(pp), n_p, n1)
    marked = (jnp.sum(jnp.where(rank1 < float(k), 1.0, 0.0), axis=0, keepdims=True)
              + jnp.sum(jnp.where(rank2 < float(k), 1.0, 0.0), axis=0, keepdims=True)
              + jnp.sum(chosen, axis=0, keepdims=True))
    tables = (n1, jnp.exp(s1 - v1[0]) / z, rank2, jnp.exp(s2 - v2[0]))
    return tables, marked - 3.0 * k


def _route_body(qr_ref, keys_ref, n1_ref, w1_ref, r2_ref, w2_ref):
    nt = (((1,), (1,)), ((), ()))
    s1 = lax.dot_general(keys_ref[0], qr_ref[:, 0:LANES], nt, preferred_element_type=F32)
    s2 = lax.dot_general(keys_ref[1], qr_ref[:, LANES:2 * LANES], nt, preferred_element_type=F32)
    out_refs = (n1_ref, w1_ref, r2_ref, w2_ref)
    tables, extra = _route_tables(s1, s2, ties=False)
    tied = jnp.max(extra) > 0.0
    for ref, t in zip(out_refs, tables):
        ref[0] = t

    @pl.when(tied)
    def _exact():
        exact_tables, _ = _route_tables(s1, s2, ties=True)
        for ref, t in zip(out_refs, exact_tables):
            ref[0] = t


def _route(qr, sub_keys, tt):
    n = qr.shape[0]
    heads, _, n_keys, d_half = sub_keys.shape
    keys = sub_keys.reshape(heads * 2, n_keys, d_half).astype(BF16)
    out = jax.ShapeDtypeStruct((heads, n_keys, n), F32)
    ospec = pl.BlockSpec((1, n_keys, tt), lambda i, h: (h, 0, i))
    return pl.pallas_call(
        _route_body,
        grid=(n // tt, heads),
        in_specs=[pl.BlockSpec((tt, 2 * d_half), lambda i, h: (i, h)),
                  pl.BlockSpec((2, n_keys, d_half), lambda i, h: (h, 0, 0))],
        out_specs=[ospec, ospec, ospec, ospec],
        out_shape=[out, out, out, out],
        compiler_params=_params("parallel", "arbitrary"),
        name="peer_route",
    )(qr, keys)


def _gelu(x):
    return 0.5 * x * (1.0 + lax.erf(x * (2.0 ** -0.5)))


def _peer_body(h_ref, u_ref, v_ref, n1_ref, w1_ref, r2_ref, w2_ref, x_ref, o_ref,
               z0_sc, z1_sc, a_sc, *, ec, tt, heads, n_keys):
    j = pl.program_id(1)
    nsl = ec // n_keys

    @pl.when(j == 0)
    def _init():
        o_ref[...] = x_ref[...]
        z1_sc[...] = jnp.zeros(z1_sc.shape, F32)

    def stage(z_next, z_prev):
        z_next[...] = lax.dot_general(u_ref[...], h_ref[...], (((1,), (1,)), ((), ())),
                                      preferred_element_type=F32)
        for sl in range(nsl):
            for tb in range(tt // LANES):
                cols = slice(tb * LANES, (tb + 1) * LANES)
                g = jnp.zeros((n_keys, LANES), F32)
                for h in range(heads):
                    row = h * nsl + sl
                    n1 = n1_ref[0, row:row + 1, cols]
                    w1 = w1_ref[0, row:row + 1, cols]
                    g = g + jnp.where(r2_ref[h, :, cols] < n1, w2_ref[h, :, cols] * w1, 0.0)
                z = z_prev[sl * n_keys:(sl + 1) * n_keys, cols]
                a_sc[sl * n_keys:(sl + 1) * n_keys, cols] = (_gelu(z) * g).astype(a_sc.dtype)
        o_ref[...] += lax.dot_general(a_sc[...], v_ref[...], (((0,), (0,)), ((), ())),
                                      preferred_element_type=F32)

    @pl.when(j % 2 == 0)
    def _even():
        stage(z0_sc, z1_sc)

    @pl.when(j % 2 == 1)
    def _odd():
        stage(z1_sc, z0_sc)


def _peer(h2, u_bf, v_bf, tables, x, tt, ec):
    n, d = h2.shape
    n_exp = u_bf.shape[0]
    n1, w1, r2, w2 = tables
    heads, n_keys, _ = n1.shape
    nsl = ec // n_keys
    nc = n_exp // ec

    def by_chunk(t):
        t = t.reshape(heads, n_keys // nsl, nsl, n)
        return jnp.transpose(t, (1, 0, 2, 3)).reshape(n_keys // nsl, heads * nsl, n)

    once = pl.Buffered(1)
    tspec = pl.BlockSpec((heads, n_keys, tt), lambda i, j: (0, 0, i), pipeline_mode=once)
    prev = lambda j: jnp.maximum(j - 1, 0)
    cspec = pl.BlockSpec((1, heads * nsl, tt), lambda i, j: (prev(j), 0, i))
    body = functools.partial(_peer_body, ec=ec, tt=tt, heads=heads, n_keys=n_keys)
    return pl.pallas_call(
        body,
        grid=(n // tt, nc + 1),
        in_specs=[pl.BlockSpec((tt, d), lambda i, j: (i, 0), pipeline_mode=once),
                  pl.BlockSpec((ec, d), lambda i, j: (jnp.minimum(j, nc - 1), 0)),
                  pl.BlockSpec((ec, d), lambda i, j: (prev(j), 0)),
                  cspec, cspec, tspec, tspec,
                  pl.BlockSpec((tt, d), lambda i, j: (i, 0), pipeline_mode=once)],
        out_specs=pl.BlockSpec((tt, d), lambda i, j: (i, 0)),
        out_shape=jax.ShapeDtypeStruct((n, d), F32),
        scratch_shapes=[pltpu.VMEM((ec, tt), F32), pltpu.VMEM((ec, tt), F32),
                        pltpu.VMEM((ec, tt), BF16)],
        compiler_params=_params("parallel", "arbitrary"),
        name="peer_experts",
    )(h2, u_bf, v_bf, by_chunk(n1), by_chunk(w1), r2, w2, x)


def _tile(n, pref):
    return pref if n % pref == 0 else n


def _project(x2d, ln_g, w_in_bf, qn_g, kn_g, widths):
    pool_w, attn_w = widths
    n, d = x2d.shape
    tm = _tile(n, 1024)
    xn = _rmsnorm(x2d, ln_g, _tile(n, 256))
    mm = functools.partial(_matmul, [xn], [(w_in_bf, 0, d)], tm=tm, tn=512, out_dtype=F32)
    u = mm(n_out=pool_w, w_col_off=0, name="proj_u")
    q = mm(n_out=attn_w, w_col_off=pool_w, gain=qn_g, qk_group=DQK, name="proj_q")
    k = mm(n_out=attn_w, w_col_off=pool_w + attn_w, gain=kn_g, qk_group=DQK, name="proj_k")
    v = mm(n_out=attn_w, w_col_off=pool_w + 2 * attn_w, name="proj_v")
    return u, q, k, v


def _mix_and_peer(x2d, pool_y, att, w_o_bf, ln2_g, w_query_bf, sub_keys, u_bf, v_bf):
    n, d = x2d.shape
    pool_w = pool_y.shape[1]
    tm = _tile(n, 1024)
    assert att.shape[1] == pool_w, "w_o row blocks assume equal pool and attention widths"
    x1 = _matmul([pool_y, att], [(w_o_bf, 0, pool_w), (w_o_bf, 1, pool_w)],
                 n_out=d, tm=tm, tn=512, out_dtype=F32, res=x2d, name="out_proj")
    h2 = _rmsnorm(x1, ln2_g, _tile(n, 256))
    qr = _matmul([h2], [(w_query_bf, 0, d)], n_out=w_query_bf.shape[1], tm=tm, tn=512,
                 out_dtype=BF16, name="peer_query")
    tables = _route(qr, sub_keys, _tile(n, 2 * LANES))
    return _peer(h2, u_bf, v_bf, tables, x1, _tile(n, 512), 512)


def kernel(x_prompt, x_sample, cache_k, cache_v, state_pool, page_table, rel_bias, ln1_g, w_in, q_norm_g, k_norm_g, lambda_q1, lambda_k1, lambda_q2, lambda_k2, subln_g, w_pool, pool_scale, w_o, ln2_g, w_query, sub_keys, peer_u, peer_v):
    batch, seq, d = x_prompt.shape
    bd, tq, _ = x_sample.shape
    assert tq == 1, "sample group is one new position per sequence"
    depth = w_in.shape[0]
    pool_w = w_pool.shape[1] * w_pool.shape[2]
    attn_w = (w_in.shape[2] - pool_w) // 3
    heads = attn_w // DV
    n_state = state_pool.shape[2]

    xp = x_prompt.reshape(batch * seq, d)
    xs = x_sample.reshape(bd, d)
    kp_l, vp_l, pp_l, ks_l, vs_l, ps_l = [], [], [], [], [], []
    for l in range(depth):
        lam_init = 0.8 - 0.6 * math.exp(-0.3 * l)
        lam_vecs = jnp.stack([lambda_q1[l], lambda_k1[l], lambda_q2[l], lambda_k2[l]]).astype(F32)
        w_in_bf = w_in[l].astype(BF16)
        w_o_bf = w_o[l].astype(BF16)
        w_query_bf = w_query[l].astype(BF16)
        u_bf = peer_u[l].astype(BF16)
        v_bf = peer_v[l].astype(BF16)

        u, q, k, v = _project(xp, ln1_g[l], w_in_bf, q_norm_g[l], k_norm_g[l], (pool_w, attn_w))
        pool_y = _pool_prompt(u, w_pool[l], pool_scale[l], batch, seq)
        att = _attn_prompt(q, k, v, lam_vecs, rel_bias, subln_g[l], batch, seq, lam_init)
        xp = _mix_and_peer(xp, pool_y, att, w_o_bf, ln2_g[l], w_query_bf, sub_keys[l], u_bf, v_bf)
        kp_l.append(k.reshape(batch, seq, heads, 2 * DQK))
        vp_l.append(v.reshape(batch, seq, heads, DV))
        pp_l.append(u.reshape(batch, seq, pool_w)[:, seq - n_state:])

        u, q, k, v = _project(xs, ln1_g[l], w_in_bf, q_norm_g[l], k_norm_g[l], (pool_w, attn_w))
        pool_y = _pool_sample(state_pool[l], u, w_pool[l], pool_scale[l])
        att = _attn_decode(q, k, v, cache_k, cache_v, l, page_table, lam_vecs, rel_bias,
                           subln_g[l], lam_init).astype(BF16)
        xs = _mix_and_peer(xs, pool_y, att, w_o_bf, ln2_g[l], w_query_bf, sub_keys[l], u_bf, v_bf)
        ks_l.append(k.reshape(bd, tq, heads, 2 * DQK))
        vs_l.append(v.reshape(bd, tq, heads, DV))
        ps_l.append(jnp.concatenate([state_pool[l], u[:, None, :]], axis=1)[:, -n_state:])

    return (xp.reshape(batch, seq, d), xs.reshape(bd, tq, d),
            jnp.stack(kp_l), jnp.stack(vp_l), jnp.stack(pp_l),
            jnp.stack(ks_l), jnp.stack(vs_l), jnp.stack(ps_l))
```
